```python
import math
import jax
import jax.numpy as jnp
from jax import lax
import numpy as np

D_MODEL = 2048
BATCH = 2
SEQ = 4096
DEPTH = 2
DEC_BATCH = 128
DEC_SEQ = 8
PAST_LEN = 16384
PAGE_SIZE = 128

N_EVEN = (DEPTH + 1) // 2
N_ODD = DEPTH // 2

H_A = 8
HKV_A = 1
DH_A = 128
REP_A = H_A // HKV_A
MOBA_BLOCK = 256
MOBA_TOPK = 3
MOBA_QCHUNK = 32
H_B = 8
DK_B = 128
DV_B = 128
GLA_CHUNK = 32
H_C = 8
DC_Q = 512
DC_KV = 512
DN_C = 128
DR_C = 64
DV_C = 128
ROPE_THETA = 10000.0
H_D = 4
HKV_D = 1
DH_D = 128
REP_D = H_D // HKV_D
D_FF = ((8 * D_MODEL + 3 * 256 - 1) // (3 * 256)) * 256

ATTN_QBLOCK = 128
NORM_EPS = 1e-6
SUBLN_EPS = 1e-5
NEG = -1e30

W0_SPLITS = (H_A * DH_A, HKV_A * DH_A, HKV_A * DH_A, H_B * DK_B, H_B * DK_B, H_B * DV_B, H_B * DV_B)
W0_IN = sum(W0_SPLITS)
W0_OUT = H_A * DH_A + H_B * DV_B
W1_SPLITS = (DC_Q, DC_KV, DR_C, H_D * 2 * DH_D, HKV_D * 2 * DH_D, HKV_D * 2 * DH_D)
W1_IN = sum(W1_SPLITS)
W1_OUT = H_C * DV_C + H_D * 2 * DH_D

kernel_name = 'hybrid_moba_hgrn2_mla_diff_decode_step'


def rmsnorm(x, g, eps=NORM_EPS):
    xf = x.astype(jnp.float32)
    y = xf * lax.rsqrt(jnp.mean(xf * xf, axis=-1, keepdims=True) + eps)
    return (y * g.astype(jnp.float32)).astype(x.dtype)


def split_cols(z, sizes):
    out, off = [], 0
    for s in sizes:
        out.append(z[..., off:off + s])
        off += s
    return out


def rope(x, pos):
    half = x.shape[-1] // 2
    inv = ROPE_THETA ** (-jnp.arange(half, dtype=jnp.float32) / half)
    ang = pos.astype(jnp.float32)[:, None] * inv[None, :]
    cos = jnp.cos(ang)[:, None, :]
    sin = jnp.sin(ang)[:, None, :]
    xf = x.astype(jnp.float32)
    x1, x2 = xf[..., :half], xf[..., half:]
    return jnp.concatenate([x1 * cos - x2 * sin, x2 * cos + x1 * sin], axis=-1).astype(x.dtype)


def swiglu(h, w_gate, w_up, w_down):
    return (jax.nn.silu(h @ w_gate) * (h @ w_up)) @ w_down


def sweep_query_blocks(fn, qs, qb):
    t = qs[0].shape[1]
    qb = min(qb, t)
    nqb = t // qb
    to_blocks = lambda a: jnp.swapaxes(a.reshape((a.shape[0], nqb, qb) + a.shape[2:]), 0, 1)
    pos = jnp.arange(t, dtype=jnp.int32).reshape(nqb, qb)
    out = lax.map(lambda args: fn(*args), tuple(to_blocks(a) for a in qs) + (pos,))
    out = jnp.swapaxes(out, 0, 1)
    return out.reshape((out.shape[0], t) + out.shape[3:])


def moba_blocks(k, v):
    tk = k.shape[0]
    nb = -(-tk // MOBA_BLOCK)
    pad = nb * MOBA_BLOCK - tk
    k = jnp.pad(k, ((0, pad), (0, 0), (0, 0))).reshape(nb, MOBA_BLOCK, HKV_A, DH_A)
    v = jnp.pad(v, ((0, pad), (0, 0), (0, 0))).reshape(nb, MOBA_BLOCK, HKV_A, DH_A)
    means = k.astype(jnp.float32).mean(axis=1)
    return k.transpose(2, 0, 1, 3), v.transpose(2, 0, 1, 3), means


def moba_attend(q, q_pos, kb, vb, means):
    f32 = jnp.float32
    tq = q.shape[0]
    nb = means.shape[0]
    qg = q.astype(f32).reshape(tq, HKV_A, REP_A, DH_A)
    own = q_pos // MOBA_BLOCK
    gate = jnp.einsum('qgrd,ngd->qgrn', qg, means)
    if nb < MOBA_TOPK:
        gate = jnp.pad(gate, ((0, 0), (0, 0), (0, 0), (0, MOBA_TOPK - nb)), constant_values=NEG)
    blk = jnp.arange(gate.shape[-1])
    gate = jnp.where(blk[None, None, None, :] < own[:, None, None, None], gate, NEG)
    _, idx = lax.top_k(gate, MOBA_TOPK)
    valid = idx < own[:, None, None, None]
    idx = jnp.where(valid, idx, 0)
    g = jnp.arange(HKV_A)
    k_sel = kb[g[None, :, None, None], idx].astype(f32)
    v_sel = vb[g[None, :, None, None], idx].astype(f32)
    k_own = kb[g[None, :], own[:, None]].astype(f32)
    v_own = vb[g[None, :], own[:, None]].astype(f32)
    scale = DH_A ** -0.5
    n_sel = MOBA_TOPK * MOBA_BLOCK
    s_sel = jnp.einsum('qgrd,qgrjbd->qgrjb', qg, k_sel) * scale
    s_sel = jnp.where(valid[..., None], s_sel, NEG).reshape(tq, HKV_A, REP_A, n_sel)
    own_pos = own[:, None] * MOBA_BLOCK + jnp.arange(MOBA_BLOCK)[None, :]
    s_own = jnp.einsum('qgrd,qgbd->qgrb', qg, k_own) * scale
    s_own = jnp.where((own_pos <= q_pos[:, None])[:, None, None, :], s_own, NEG)
    p = jax.nn.softmax(jnp.concatenate([s_sel, s_own], axis=-1), axis=-1)
    p_sel = p[..., :n_sel].reshape(tq, HKV_A, REP_A, MOBA_TOPK, MOBA_BLOCK)
    o = jnp.einsum('qgrjb,qgrjbd->qgrd', p_sel, v_sel) + jnp.einsum('qgrb,qgbd->qgrd', p[..., n_sel:], v_own)
    return o.reshape(tq, H_A, DH_A).astype(q.dtype)


def moba_prompt(q, k, v):
    kb, vb, means = jax.vmap(moba_blocks)(k, v)
    attend = jax.vmap(moba_attend, in_axes=(0, None, 0, 0, 0))
    return sweep_query_blocks(lambda qq, p: attend(qq, p, kb, vb, means), (q,), MOBA_QCHUNK)


def moba_sample(q, k_new, v_new, cache_k, cache_v, li, page_table, q_pos):
    def one(args):
        qq, kn, vn, pt = args
        kp = cache_k[li, pt].reshape((-1,) + cache_k.shape[3:]).astype(kn.dtype)
        vp = cache_v[li, pt].reshape((-1,) + cache_v.shape[3:]).astype(vn.dtype)
        kb, vb, means = moba_blocks(jnp.concatenate([kp, kn], 0), jnp.concatenate([vp, vn], 0))
        return moba_attend(qq, q_pos, kb, vb, means)
    return lax.map(one, (q, k_new, v_new, page_table))


def gla_chunked(q, k, v, log_f, s0):
    n, t, h, _ = q.shape
    c = min(GLA_CHUNK, t)
    n_chunks = -(-t // c)
    pad = n_chunks * c - t

    def to_chunks(a):
        a = jnp.pad(a, ((0, 0), (0, pad), (0, 0), (0, 0)))
        return jnp.swapaxes(a.reshape(n, n_chunks, c, h, a.shape[-1]), 0, 1)

    causal = jnp.tril(jnp.ones((c, c), dtype=bool))[None, :, :, None, None]

    def step(s, xs):
        qc, kc, vc, gc = xs
        b = jnp.cumsum(gc, axis=1)
        decay = jnp.exp(jnp.where(causal, b[:, :, None] - b[:, None, :], -jnp.inf))
        att = jnp.einsum('nthk,nshk,ntshk->nhts', qc, kc, decay)
        o = jnp.einsum('nhts,nshv->nthv', att, vc) + jnp.einsum('nthk,nhkv->nthv', qc * jnp.exp(b), s)
        b_last = b[:, -1]
        s = s * jnp.exp(b_last)[..., None] + jnp.einsum('nshk,nshv->nhkv', kc * jnp.exp(b_last[:, None] - b), vc)
        return s, o

    s, o = lax.scan(step, s0, (to_chunks(q), to_chunks(k), to_chunks(v), to_chunks(log_f)))
    o = jnp.swapaxes(o, 0, 1).reshape(n, n_chunks * c, h, v.shape[-1])[:, :t]
    return o, s


def hgrn2(q_raw, f_raw, i_raw, g_raw, lb, s0, gnorm):
    n, t, _ = q_raw.shape
    f32 = jnp.float32
    heads = lambda a, d: a.astype(f32).reshape(n, t, H_B, d)
    lb = lb.reshape(H_B, DK_B)
    q = jax.nn.silu(heads(q_raw, DK_B))
    fg = lb + (1.0 - lb) * jax.nn.sigmoid(heads(f_raw, DK_B))
    o, s = gla_chunked(q, 1.0 - fg, heads(i_raw, DV_B), jnp.log(fg), s0)
    o = rmsnorm(o, gnorm) * jax.nn.silu(heads(g_raw, DV_B))
    return o.reshape(n, t, H_B * DV_B).astype(q_raw.dtype), s


def mla_core(q_lat, q_pe, ckv, kpe, q_pos, k_pos):
    f32 = jnp.float32
    s = (jnp.einsum('nqhc,nkc->nhqk', q_lat.astype(f32), ckv.astype(f32))
         + jnp.einsum('nqhr,nkr->nhqk', q_pe.astype(f32), kpe.astype(f32))) * (DN_C + DR_C) ** -0.5
    s = jnp.where(k_pos[None, :] <= q_pos[:, None], s, NEG)
    p = jax.nn.softmax(s, axis=-1)
    return jnp.einsum('nhqk,nkc->nqhc', p, ckv.astype(f32)).astype(q_lat.dtype)


def diff_core(q, k, v, lam, q_pos, k_pos):
    f32 = jnp.float32
    n, tk = k.shape[:2]
    kk = k.astype(f32).reshape(n, tk, HKV_D, 2, DH_D)
    s = jnp.einsum('nqgrjd,nkgjd->njgrqk', q.astype(f32), kk) * DH_D ** -0.5
    s = jnp.where(k_pos[None, :] <= q_pos[:, None], s, NEG)
    p = jax.nn.softmax(s, axis=-1)
    a = p[:, 0] - lam * p[:, 1]
    return jnp.einsum('ngrqk,nkgv->nqgrv', a, v.astype(f32))


def even_mixer(h, pos, li, w_in, w_out, lb, gnorm, past):
    n, t, _ = h.shape
    aq, ak, av, bq, bf, bi, bg = split_cols(h @ w_in, W0_SPLITS)
    aq = aq.reshape(n, t, H_A, DH_A)
    ak = ak.reshape(n, t, HKV_A, DH_A)
    av = av.reshape(n, t, HKV_A, DH_A)
    if past is None:
        oa = moba_prompt(aq, ak, av)
        s0 = jnp.zeros((n, H_B, DK_B, DV_B), jnp.float32)
    else:
        cache_k, cache_v, state, page_table = past
        oa = moba_sample(aq, ak, av, cache_k, cache_v, li, page_table, pos)
        s0 = state[li].astype(jnp.float32)
    ob, s = hgrn2(bq, bf, bi, bg, lb, s0, gnorm)
    y = jnp.concatenate([oa.reshape(n, t, H_A * DH_A), ob], axis=-1) @ w_out
    return y, ak, av, s.astype(h.dtype)


def odd_mixer(h, pos, li, w_in, w_out, q_norm, kv_norm, w_uq, w_uk, w_uv, lq1, lk1, lq2, lk2, subln, lam_init, past):
    n, t, _ = h.shape
    f32 = jnp.float32
    cq, ckv, kpe, dq, dk, dv = split_cols(h @ w_in, W1_SPLITS)
    qc = (rmsnorm(cq, q_norm) @ w_uq).reshape(n, t, H_C, DN_C + DR_C)
    q_pe = rope(qc[..., DN_C:], pos)
    q_lat = jnp.einsum('nthd,chd->nthc', qc[..., :DN_C], w_uk)
    ckv_n = rmsnorm(ckv, kv_norm)
    kpe_r = rope(kpe[:, :, None, :], pos)[:, :, 0, :]
    dq6 = dq.reshape(n, t, HKV_D, REP_D, 2, DH_D)
    dk4 = dk.reshape(n, t, HKV_D, 2 * DH_D)
    dv4 = dv.reshape(n, t, HKV_D, 2 * DH_D)
    lam = (jnp.exp(jnp.sum(lq1.astype(f32) * lk1.astype(f32)))
           - jnp.exp(jnp.sum(lq2.astype(f32) * lk2.astype(f32))) + lam_init)
    if past is None:
        k_pos = jnp.arange(t, dtype=jnp.int32)
        o_lat = sweep_query_blocks(lambda a, b, p: mla_core(a, b, ckv_n, kpe_r, p, k_pos), (q_lat, q_pe), ATTN_QBLOCK)
        o_d = sweep_query_blocks(lambda a, p: diff_core(a, dk4, dv4, lam, p, k_pos), (dq6,), ATTN_QBLOCK)
    else:
        cache_cl, cache_cr, cache_dk, cache_dv, page_table = past
        k_pos = jnp.arange(page_table.shape[1] * PAGE_SIZE + t, dtype=jnp.int32)

        def one(args):
            ql, qp, cn, kr, dqq, dkk, dvv, pt = args
            gather = lambda c, new: jnp.concatenate(
                [c[li, pt].reshape((-1,) + c.shape[3:]).astype(new.dtype), new], axis=0)[None]
            ol = mla_core(ql[None], qp[None], gather(cache_cl, cn), gather(cache_cr, kr), pos, k_pos)[0]
            od = diff_core(dqq[None], gather(cache_dk, dkk), gather(cache_dv, dvv), lam, pos, k_pos)[0]
            return ol, od

        o_lat, o_d = lax.map(one, (q_lat, q_pe, ckv_n, kpe_r, dq6, dk4, dv4, page_table))
    oc = jnp.einsum('nthc,chv->nthv', o_lat, w_uv).reshape(n, t, H_C * DV_C)
    od = (rmsnorm(o_d, subln, SUBLN_EPS) * (1.0 - lam_init)).reshape(n, t, H_D * 2 * DH_D).astype(h.dtype)
    y = jnp.concatenate([oc.astype(h.dtype), od], axis=-1) @ w_out
    return y, ckv_n, kpe_r, dk4, dv4


def setup_inputs(seed: int = 0) -> dict:
    key = jax.random.key(seed)
    keys = iter(jax.random.split(key, 48))
    f32 = jnp.float32

    def nrm(shape, scale):
        return jax.random.normal(next(keys), shape, f32) * scale

    def gain(shape):
        return 1.0 + nrm(shape, 0.02)

    n_pages = PAST_LEN // PAGE_SIZE
    n_pool = (5 * DEC_BATCH * n_pages + 3) // 4
    page_table = jax.random.permutation(next(keys), n_pool)[: DEC_BATCH * n_pages].reshape(DEC_BATCH, n_pages).astype(jnp.int32)
    return {
        'x_prompt': nrm((BATCH, SEQ, D_MODEL), 1.0),
        'x_sample': nrm((DEC_BATCH, DEC_SEQ, D_MODEL), 1.0),
        'cache_a_k': nrm((N_EVEN, n_pool, PAGE_SIZE, HKV_A, DH_A), 1.0),
        'cache_a_v': nrm((N_EVEN, n_pool, PAGE_SIZE, HKV_A, DH_A), 1.0),
        'state_b': nrm((N_EVEN, DEC_BATCH, H_B, DK_B, DV_B), 0.5),
        'cache_c_latent': nrm((N_ODD, n_pool, PAGE_SIZE, DC_KV), 1.0),
        'cache_c_krope': nrm((N_ODD, n_pool, PAGE_SIZE, DR_C), 1.0),
        'cache_d_k': nrm((N_ODD, n_pool, PAGE_SIZE, HKV_D, 2 * DH_D), 1.0),
        'cache_d_v': nrm((N_ODD, n_pool, PAGE_SIZE, HKV_D, 2 * DH_D), 1.0),
        'page_table': page_table,
        'norm_mix': gain((DEPTH, D_MODEL)),
        'norm_ffn': gain((DEPTH, D_MODEL)),
        'norm_final': gain((D_MODEL,)),
        'w_in_even': nrm((N_EVEN, D_MODEL, W0_IN), D_MODEL ** -0.5),
        'w_out_even': nrm((N_EVEN, W0_OUT, D_MODEL), W0_OUT ** -0.5),
        'hgrn_lb': nrm((N_EVEN + 1, H_B * DK_B), 0.1),
        'hgrn_gnorm': gain((N_EVEN, DV_B)),
        'w_in_odd': nrm((N_ODD, D_MODEL, W1_IN), D_MODEL ** -0.5),
        'w_out_odd': nrm((N_ODD, W1_OUT, D_MODEL), W1_OUT ** -0.5),
        'mla_q_norm': gain((N_ODD, DC_Q)),
        'mla_kv_norm': gain((N_ODD, DC_KV)),
        'w_mla_uq': nrm((N_ODD, DC_Q, H_C * (DN_C + DR_C)), DC_Q ** -0.5),
        'w_mla_uk': nrm((N_ODD, DC_KV, H_C, DN_C), DC_KV ** -0.5),
        'w_mla_uv': nrm((N_ODD, DC_KV, H_C, DV_C), DC_KV ** -0.5),
        'diff_lq1': nrm((N_ODD, DH_D), 0.1),
        'diff_lk1': nrm((N_ODD, DH_D), 0.1),
        'diff_lq2': nrm((N_ODD, DH_D), 0.1),
        'diff_lk2': nrm((N_ODD, DH_D), 0.1),
        'diff_subln': gain((N_ODD, 2 * DH_D)),
        'w_ffn_gate': nrm((DEPTH, D_MODEL, D_FF), D_MODEL ** -0.5),
        'w_ffn_up': nrm((DEPTH, D_MODEL, D_FF), D_MODEL ** -0.5),
        'w_ffn_down': nrm((DEPTH, D_FF, D_MODEL), D_FF ** -0.5),
    }


def reference(x_prompt, x_sample, cache_a_k, cache_a_v, state_b, cache_c_latent, cache_c_krope, cache_d_k, cache_d_v,
              page_table, norm_mix, norm_ffn, norm_final, w_in_even, w_out_even, hgrn_lb, hgrn_gnorm, w_in_odd,
              w_out_odd, mla_q_norm, mla_kv_norm, w_mla_uq, w_mla_uk, w_mla_uv, diff_lq1, diff_lk1, diff_lq2,
              diff_lk2, diff_subln, w_ffn_gate, w_ffn_up, w_ffn_down):
    pos_p = jnp.arange(SEQ, dtype=jnp.int32)
    pos_s = PAST_LEN + jnp.arange(DEC_SEQ, dtype=jnp.int32)
    lower_bounds = jnp.cumsum(jax.nn.softmax(hgrn_lb.astype(jnp.float32), axis=0), axis=0)
    xp, xs = x_prompt, x_sample
    even_p, even_s, odd_p, odd_s = [], [], [], []
    for l in range(DEPTH):
        li = l // 2
        if l % 2 == 0:
            w = (w_in_even[li], w_out_even[li], lower_bounds[li], hgrn_gnorm[li])
            yp, *sp = even_mixer(rmsnorm(xp, norm_mix[l]), pos_p, li, *w, None)
            ys, *ss = even_mixer(rmsnorm(xs, norm_mix[l]), pos_s, li, *w, (cache_a_k, cache_a_v, state_b, page_table))
            even_p.append(sp)
            even_s.append(ss)
        else:
            lam_init = 0.8 - 0.6 * math.exp(-0.3 * l)
            w = (w_in_odd[li], w_out_odd[li], mla_q_norm[li], mla_kv_norm[li], w_mla_uq[li], w_mla_uk[li],
                 w_mla_uv[li], diff_lq1[li], diff_lk1[li], diff_lq2[li], diff_lk2[li], diff_subln[li], lam_init)
            yp, *sp = odd_mixer(rmsnorm(xp, norm_mix[l]), pos_p, li, *w, None)
            ys, *ss = odd_mixer(rmsnorm(xs, norm_mix[l]), pos_s, li, *w,
                                (cache_c_latent, cache_c_krope, cache_d_k, cache_d_v, page_table))
            odd_p.append(sp)
            odd_s.append(ss)
        xp = xp + yp
        xs = xs + ys
        xp = xp + swiglu(rmsnorm(xp, norm_ffn[l]), w_ffn_gate[l], w_ffn_up[l], w_ffn_down[l])
        xs = xs + swiglu(rmsnorm(xs, norm_ffn[l]), w_ffn_gate[l], w_ffn_up[l], w_ffn_down[l])
    y_prompt = rmsnorm(xp, norm_final)
    y_sample = rmsnorm(xs, norm_final)
    stack = lambda rows, j: jnp.stack([r[j] for r in rows], axis=0)
    a_k_prompt, a_v_prompt, b_state_prompt = stack(even_p, 0), stack(even_p, 1), stack(even_p, 2)
    a_k_sample, a_v_sample, b_state_sample = stack(even_s, 0), stack(even_s, 1), stack(even_s, 2)
    c_latent_prompt, c_krope_prompt = stack(odd_p, 0), stack(odd_p, 1)
    d_k_prompt, d_v_prompt = stack(odd_p, 2), stack(odd_p, 3)
    c_latent_sample, c_krope_sample = stack(odd_s, 0), stack(odd_s, 1)
    d_k_sample, d_v_sample = stack(odd_s, 2), stack(odd_s, 3)
    return (y_prompt, y_sample, a_k_prompt, a_v_prompt, b_state_prompt, c_latent_prompt, c_krope_prompt,
            d_k_prompt, d_v_prompt, a_k_sample, a_v_sample, b_state_sample, c_latent_sample, c_krope_sample,
            d_k_sample, d_v_sample)
```

```python
import functools
import math

import jax
import jax.numpy as jnp
from jax import lax
from jax.experimental import pallas as pl
from jax.experimental.pallas import tpu as pltpu

F32 = jnp.float32
BF16 = jnp.bfloat16
HIGHEST = lax.Precision.HIGHEST

PAGE_SIZE = 128
H_A, DH_A = 8, 128
MOBA_BLOCK, MOBA_TOPK = 256, 3
H_B, DK_B, DV_B = 8, 128, 128
H_C, DC_Q, DC_KV, DN_C, DR_C, DV_C = 8, 512, 512, 128, 64, 128
ROPE_THETA = 10000.0
H_D, DH_D = 4, 128
NORM_EPS = 1e-6
SUBLN_EPS = 1e-5
NEG = -1e30
BELOW_NEG = -3e38

VMEM_LIMIT_BYTES = 56 * 1024 * 1024
SUBLANES = 8
LANES = 128
BF16_ROWS = 16


def _cparams(*sem):
    return pltpu.CompilerParams(dimension_semantics=sem, vmem_limit_bytes=VMEM_LIMIT_BYTES)


def _tile(n, pref, mult):
    if n <= pref:
        return n
    t = (pref // mult) * mult
    while t >= mult:
        if n % t == 0:
            return t
        t -= mult
    raise ValueError(f"no tile for {n} (pref {pref}, mult {mult})")


def _dot_nt(a, b, precision=None):
    return lax.dot_general(a, b, (((1,), (1,)), ((), ())), preferred_element_type=F32, precision=precision)


def _dot_tn(a, b, precision=None):
    return lax.dot_general(a, b, (((0,), (0,)), ((), ())), preferred_element_type=F32, precision=precision)


def _sigmoid(x):
    return 1.0 / (1.0 + jnp.exp(-x))


def _rms(x, g, eps):
    return x * lax.rsqrt(jnp.mean(x * x, axis=-1, keepdims=True) + eps) * g


def _pad_rows(x, n):
    if x.shape[0] >= n:
        return x
    return jnp.concatenate([x, jnp.zeros((n - x.shape[0], x.shape[1]), x.dtype)], axis=0)


def _online_softmax(s, m_ref, l_ref):
    m_prev = m_ref[...]
    m_new = jnp.maximum(m_prev, jnp.max(s, axis=1, keepdims=True))
    alpha = jnp.exp(m_prev - m_new)
    p = jnp.exp(s - m_new)
    l_ref[...] = alpha * l_ref[...] + jnp.sum(p, axis=1, keepdims=True)
    m_ref[...] = m_new
    return alpha, p


def _norm_matmul_body(x_ref, g_ref, w_ref, o_ref, h_ref):
    @pl.when(pl.program_id(1) == 0)
    def _():
        h_ref[...] = _rms(x_ref[...], g_ref[...], NORM_EPS).astype(BF16)

    o_ref[...] = jnp.dot(h_ref[...], w_ref[...], preferred_element_type=F32)


def norm_matmul(x, g, w):
    m, d = x.shape
    n = w.shape[1]
    tm = _tile(m, 512, SUBLANES)
    tn = _tile(n, 896, LANES)
    return pl.pallas_call(
        _norm_matmul_body,
        grid=(m // tm, n // tn),
        in_specs=[pl.BlockSpec((tm, d), lambda i, j: (i, 0)),
                  pl.BlockSpec((1, d), lambda i, j: (0, 0)),
                  pl.BlockSpec((d, tn), lambda i, j: (0, j))],
        out_specs=pl.BlockSpec((tm, tn), lambda i, j: (i, j)),
        out_shape=jax.ShapeDtypeStruct((m, n), F32),
        scratch_shapes=[pltpu.VMEM((tm, d), BF16)],
        compiler_params=_cparams("parallel", "arbitrary"),
        name="norm_matmul",
    )(x, g.reshape(1, d), w)


def _proj_residual_body(x_ref, a1_ref, a2_ref, w1_ref, w2_ref, o_ref):
    y = jnp.dot(a1_ref[...].astype(BF16), w1_ref[...], preferred_element_type=F32)
    y = y + jnp.dot(a2_ref[...].astype(BF16), w2_ref[...], preferred_element_type=F32)
    o_ref[...] = x_ref[...] + y


def proj_residual(x, a1, a2, w):
    m, d = x.shape
    k1, k2 = a1.shape[1], a2.shape[1]
    assert w.shape == (k1 + k2, d) and k1 == k2
    tm = _tile(m, 512, SUBLANES)
    tn = _tile(d, 1024, LANES)
    return pl.pallas_call(
        _proj_residual_body,
        grid=(m // tm, d // tn),
        in_specs=[pl.BlockSpec((tm, tn), lambda i, j: (i, j)),
                  pl.BlockSpec((tm, k1), lambda i, j: (i, 0)),
                  pl.BlockSpec((tm, k2), lambda i, j: (i, 0)),
                  pl.BlockSpec((k1, tn), lambda i, j: (0, j)),
                  pl.BlockSpec((k2, tn), lambda i, j: (1, j))],
        out_specs=pl.BlockSpec((tm, tn), lambda i, j: (i, j)),
        out_shape=jax.ShapeDtypeStruct((m, d), F32),
        compiler_params=_cparams("parallel", "arbitrary"),
        name="proj_residual",
    )(x, a1, a2, w, w)


def _ffn_body(x_ref, g_ref, gf_ref, wg_ref, wu_ref, wd_ref, o_ref, h_ref, acc_ref, *, final_norm):
    j = pl.program_id(1)

    @pl.when(j == 0)
    def _():
        h_ref[...] = _rms(x_ref[...], g_ref[...], NORM_EPS).astype(BF16)
        acc_ref[...] = jnp.zeros_like(acc_ref)

    h = h_ref[...]
    gate = jnp.dot(h, wg_ref[...], preferred_element_type=F32)
    up = jnp.dot(h, wu_ref[...], preferred_element_type=F32)
    act = (gate * _sigmoid(gate) * up).astype(BF16)
    acc_ref[...] += jnp.dot(act, wd_ref[...], preferred_element_type=F32)

    @pl.when(j == pl.num_programs(1) - 1)
    def _():
        y = x_ref[...] + acc_ref[...]
        if final_norm:
            y = _rms(y, gf_ref[...], NORM_EPS)
        o_ref[...] = y


def ffn(x, g, gf, wg, wu, wd, *, final_norm):
    m, d = x.shape
    f = wg.shape[1]
    tm = _tile(m, 512, SUBLANES)
    tf = _tile(f, 512, LANES)
    return pl.pallas_call(
        functools.partial(_ffn_body, final_norm=final_norm),
        grid=(m // tm, f // tf),
        in_specs=[pl.BlockSpec((tm, d), lambda i, j: (i, 0)),
                  pl.BlockSpec((1, d), lambda i, j: (0, 0)),
                  pl.BlockSpec((1, d), lambda i, j: (0, 0)),
                  pl.BlockSpec((d, tf), lambda i, j: (0, j)),
                  pl.BlockSpec((d, tf), lambda i, j: (0, j)),
                  pl.BlockSpec((tf, d), lambda i, j: (j, 0))],
        out_specs=pl.BlockSpec((tm, d), lambda i, j: (i, 0)),
        out_shape=jax.ShapeDtypeStruct((m, d), F32),
        scratch_shapes=[pltpu.VMEM((tm, d), BF16), pltpu.VMEM((tm, d), F32)],
        compiler_params=_cparams("parallel", "arbitrary"),
        name="ffn",
    )(x, g.reshape(1, d), gf.reshape(1, d), wg, wu, wd)


def _moba_means_body(k_ref, o_ref):
    t, dh = k_ref.shape
    k = k_ref[...].reshape(t // MOBA_BLOCK, MOBA_BLOCK, dh)
    o_ref[0] = jnp.sum(k, axis=1) * (1.0 / MOBA_BLOCK)


def moba_means(z, batch, seq, k_col):
    nb = seq // MOBA_BLOCK
    return pl.pallas_call(
        _moba_means_body,
        grid=(batch,),
        in_specs=[pl.BlockSpec((seq, DH_A), lambda b: (b, k_col // DH_A))],
        out_specs=pl.BlockSpec((1, nb, DH_A), lambda b: (b, 0, 0)),
        out_shape=jax.ShapeDtypeStruct((batch, nb, DH_A), F32),
        compiler_params=_cparams("parallel"),
        name="moba_means",
    )(z)


def _top_blocks(gate, n_valid):
    nb = gate.shape[1]
    col = lax.broadcasted_iota(jnp.int32, gate.shape, 1)
    colf = col.astype(F32)
    valid = col < n_valid
    g = jnp.where(valid, gate, NEG)
    sel = jnp.zeros(gate.shape, F32)
    for _ in range(MOBA_TOPK):
        mx = jnp.max(g, axis=1, keepdims=True)
        idx = jnp.min(jnp.where(g == mx, colf, float(nb)), axis=1, keepdims=True)
        pick = colf == idx
        sel = jnp.where(jnp.logical_and(pick, valid), 1.0, sel)
        g = jnp.where(pick, BELOW_NEG, g)
    return sel


def _stack_heads(q_ref, n_heads, width):
    return jnp.concatenate([q_ref[:, h * width:(h + 1) * width] for h in range(n_heads)], axis=0)


def _unstack_heads(o, o_ref, n_heads, width):
    tq = o_ref.shape[0]
    for h in range(n_heads):
        o_ref[:, h * width:(h + 1) * width] = o[h * tq:(h + 1) * tq]


def _moba_prompt_body(q_ref, k_ref, v_ref, mean_ref, o_ref, qs_ref, sel_ref, m_ref, l_ref, acc_ref):
    i = pl.program_id(1)
    jj = pl.program_id(2)
    tq = MOBA_BLOCK
    scale = DH_A ** -0.5

    @pl.when(jj == 0)
    def _init():
        qf = _stack_heads(q_ref, H_A, DH_A)
        qs_ref[...] = qf.astype(BF16)
        gate = _dot_nt(qf, mean_ref[0], precision=HIGHEST)
        sel_ref[...] = _top_blocks(gate, i)
        m_ref[...] = jnp.full(m_ref.shape, NEG, F32)
        l_ref[...] = jnp.zeros_like(l_ref)
        acc_ref[...] = jnp.zeros_like(acc_ref)

    def step(ok):
        s = _dot_nt(qs_ref[...], k_ref[...].astype(BF16)) * scale
        s = jnp.where(ok, s, NEG)
        alpha, p = _online_softmax(s, m_ref, l_ref)
        acc_ref[...] = alpha * acc_ref[...] + jnp.dot(p.astype(BF16), v_ref[...].astype(BF16),
                                                      preferred_element_type=F32)

    @pl.when(jj == 0)
    def _own():
        shape = (H_A * tq, tq)
        tok = lax.broadcasted_iota(jnp.int32, shape, 0) % tq
        key = lax.broadcasted_iota(jnp.int32, shape, 1)
        step(key <= tok)

    @pl.when(jnp.logical_and(jj > 0, jj <= i))
    def _past():
        n = i - jj
        blk = lax.broadcasted_iota(jnp.int32, sel_ref.shape, 1)
        picked = jnp.sum(jnp.where(blk == n, sel_ref[...], 0.0), axis=1, keepdims=True)
        step(picked > 0.5)

    @pl.when(jj == pl.num_programs(2) - 1)
    def _fin():
        _unstack_heads(acc_ref[...] / l_ref[...], o_ref, H_A, DH_A)


def moba_prompt(z, means, batch, seq, q_col, k_col, v_col):
    nb = seq // MOBA_BLOCK
    rows = H_A * MOBA_BLOCK
    wq = H_A * DH_A
    kv_row = lambda b, i, jj: b * nb + jnp.maximum(i - jj, 0)
    return pl.pallas_call(
        _moba_prompt_body,
        grid=(batch, nb, nb),
        in_specs=[pl.BlockSpec((MOBA_BLOCK, wq), lambda b, i, jj: (b * nb + i, q_col // wq)),
                  pl.BlockSpec((MOBA_BLOCK, DH_A), lambda b, i, jj: (kv_row(b, i, jj), k_col // DH_A)),
                  pl.BlockSpec((MOBA_BLOCK, DH_A), lambda b, i, jj: (kv_row(b, i, jj), v_col // DH_A)),
                  pl.BlockSpec((1, nb, DH_A), lambda b, i, jj: (b, 0, 0))],
        out_specs=pl.BlockSpec((MOBA_BLOCK, wq), lambda b, i, jj: (b * nb + i, 0)),
        out_shape=jax.ShapeDtypeStruct((batch * seq, wq), F32),
        scratch_shapes=[pltpu.VMEM((rows, DH_A), BF16), pltpu.VMEM((rows, nb), F32),
                        pltpu.VMEM((rows, 1), F32), pltpu.VMEM((rows, 1), F32),
                        pltpu.VMEM((rows, DH_A), F32)],
        compiler_params=_cparams("parallel", "parallel", "arbitrary"),
        name="moba_prompt",
    )(z, z, z, means)


def _new_token_scores(s_new, dec_seq):
    rows = s_new.shape[0]
    tok = lax.broadcasted_iota(jnp.int32, (rows, LANES), 0) % dec_seq
    key = lax.broadcasted_iota(jnp.int32, (rows, LANES), 1)
    return jnp.where(key <= tok, s_new, NEG)


def _new_token_values(p_new, v_new):
    acc = p_new[:, 0:1] * v_new[0:1]
    for c in range(1, v_new.shape[0]):
        acc = acc + p_new[:, c:c + 1] * v_new[c:c + 1]
    return acc


def _moba_sample_body(pt_ref, q_ref, kn_ref, vn_ref, *refs, pages_per_step, n_steps):
    del pt_ref
    pp = pages_per_step
    k_refs, v_refs = refs[:pp], refs[pp:2 * pp]
    o_ref, s_ref, bsum_ref, l_ref, acc_ref = refs[2 * pp:]
    j = pl.program_id(1)
    dec_seq = q_ref.shape[0]
    scale = DH_A ** -0.5
    rows = H_A * dec_seq
    pages_per_blk = MOBA_BLOCK // PAGE_SIZE
    blks_per_step = pp // pages_per_blk
    n_blk = bsum_ref.shape[0]

    @pl.when(j < n_steps)
    def _keys():
        q = _stack_heads(q_ref, H_A, DH_A).astype(BF16)
        for c in range(blks_per_step):
            bs = jnp.zeros((1, DH_A), F32)
            for u in range(pages_per_blk):
                pg = c * pages_per_blk + u
                k = k_refs[pg][...]
                s_ref[j * pp + pg] = _dot_nt(q, k.astype(BF16))
                bs = bs + jnp.sum(k, axis=0, keepdims=True)
            bsum_ref[pl.ds(j * blks_per_step + c, 1), :] = bs

    @pl.when(j == n_steps)
    def _softmax():
        qf = _stack_heads(q_ref, H_A, DH_A)
        gate = _dot_nt(qf, bsum_ref[...] * (1.0 / MOBA_BLOCK), precision=HIGHEST)
        sel = _top_blocks(gate, n_blk)
        k_new = _pad_rows(kn_ref[...], LANES).astype(BF16)
        s_new = _new_token_scores(_dot_nt(qf.astype(BF16), k_new) * scale, dec_seq)
        mx = s_new
        for n in range(n_blk):
            ok = sel[:, n:n + 1] > 0.5
            for u in range(pages_per_blk):
                pg = n * pages_per_blk + u
                sm = jnp.where(ok, s_ref[pg] * scale, NEG)
                s_ref[pg] = sm
                mx = jnp.maximum(mx, sm)
        m = jnp.max(mx, axis=1, keepdims=True)
        p_new = jnp.exp(s_new - m)
        lsum = p_new
        for pg in range(n_blk * pages_per_blk):
            p = jnp.exp(s_ref[pg] - m)
            s_ref[pg] = p
            lsum = lsum + p
        l_ref[...] = jnp.sum(lsum, axis=1, keepdims=True)
        acc_ref[...] = _new_token_values(p_new, vn_ref[...])

    @pl.when(j >= n_steps)
    def _values():
        jv = j - n_steps
        acc = acc_ref[...]
        for pg in range(pp):
            acc = acc + jnp.dot(s_ref[jv * pp + pg].astype(BF16), v_refs[pg][...].astype(BF16),
                                preferred_element_type=F32)
        acc_ref[...] = acc

    @pl.when(j == 2 * n_steps - 1)
    def _fin():
        _unstack_heads(acc_ref[...] / l_ref[...], o_ref, H_A, DH_A)


def moba_sample(z, cache_k, cache_v, li, page_table, row0, dec_seq, q_col, k_col, v_col):
    n_seq, n_pages = page_table.shape
    assert (n_pages * PAGE_SIZE) % MOBA_BLOCK == 0 and dec_seq <= MOBA_BLOCK and row0 % dec_seq == 0
    pp = _tile(n_pages, 16, MOBA_BLOCK // PAGE_SIZE)
    n_steps = n_pages // pp
    n_blk = n_pages * PAGE_SIZE // MOBA_BLOCK
    rb0 = row0 // dec_seq
    rows = H_A * dec_seq
    wq = H_A * DH_A
    ck = cache_k.reshape(cache_k.shape[0], cache_k.shape[1], PAGE_SIZE, DH_A)
    cv = cache_v.reshape(cache_v.shape[0], cache_v.shape[1], PAGE_SIZE, DH_A)

    def k_map(u):
        return lambda s, j, pt: (li, pt[s, jnp.minimum(j, n_steps - 1) * pp + u], 0, 0)

    def v_map(u):
        return lambda s, j, pt: (li, pt[s, jnp.maximum(j - n_steps, 0) * pp + u], 0, 0)

    page_spec = lambda f: pl.BlockSpec((None, None, PAGE_SIZE, DH_A), f)
    grid_spec = pltpu.PrefetchScalarGridSpec(
        num_scalar_prefetch=1,
        grid=(n_seq, 2 * n_steps),
        in_specs=[pl.BlockSpec((dec_seq, wq), lambda s, j, pt: (rb0 + s, q_col // wq)),
                  pl.BlockSpec((dec_seq, DH_A), lambda s, j, pt: (rb0 + s, k_col // DH_A)),
                  pl.BlockSpec((dec_seq, DH_A), lambda s, j, pt: (rb0 + s, v_col // DH_A))]
                 + [page_spec(k_map(u)) for u in range(pp)] + [page_spec(v_map(u)) for u in range(pp)],
        out_specs=pl.BlockSpec((dec_seq, wq), lambda s, j, pt: (s, 0)),
        scratch_shapes=[pltpu.VMEM((n_pages, rows, PAGE_SIZE), F32),
                        pltpu.VMEM((n_blk, DH_A), F32), pltpu.VMEM((rows, 1), F32),
                        pltpu.VMEM((rows, DH_A), F32)],
    )
    return pl.pallas_call(
        functools.partial(_moba_sample_body, pages_per_step=pp, n_steps=n_steps),
        grid_spec=grid_spec,
        out_shape=jax.ShapeDtypeStruct((n_seq * dec_seq, wq), F32),
        compiler_params=_cparams("parallel", "arbitrary"),
        name="moba_sample",
    )(page_table, z, z, z, *([ck] * pp), *([cv] * pp))


def _split3(x):
    hi = x.astype(BF16)
    r1 = x - hi.astype(F32)
    mid = r1.astype(BF16)
    lo = (r1 - mid.astype(F32)).astype(BF16)
    return hi, mid, lo


def _cumsum_rows(x):
    n = x.shape[0]
    x = _pad_rows(x, LANES)
    r = lax.broadcasted_iota(jnp.int32, (n, LANES), 0)
    c = lax.broadcasted_iota(jnp.int32, (n, LANES), 1)
    tril = jnp.where(c <= r, 1.0, 0.0).astype(BF16)
    hi, mid, lo = _split3(x)
    return (jnp.dot(tril, hi, preferred_element_type=F32) + jnp.dot(tril, mid, preferred_element_type=F32)
            + jnp.dot(tril, lo, preferred_element_type=F32))


def _gla_chunk(q, kk, v, b, st):
    c = q.shape[0]
    o_inter = _dot_nt(_pad_rows(q * jnp.exp(b), BF16_ROWS).astype(BF16), st.astype(BF16))[:c]
    n_grp = c // SUBLANES
    row = lax.broadcasted_iota(jnp.int32, (SUBLANES, 1), 0)
    parts = [o_inter[g * SUBLANES:(g + 1) * SUBLANES] for g in range(n_grp)]
    for s in range(c):
        b_s, k_s, v_s = b[s:s + 1], kk[s:s + 1], v[s:s + 1]
        for g in range(s // SUBLANES, n_grp):
            sl = slice(g * SUBLANES, (g + 1) * SUBLANES)
            d = b[sl] - b_s
            if g == s // SUBLANES:
                d = jnp.where(row >= s - g * SUBLANES, d, -jnp.inf)
            att = jnp.sum(q[sl] * jnp.exp(d) * k_s, axis=-1, keepdims=True)
            parts[g] = parts[g] + att * v_s
    o = jnp.concatenate(parts, axis=0) if n_grp > 1 else parts[0]
    b_last = b[c - 1:c]
    kd = kk * jnp.exp(b_last - b)
    st_new = st * jnp.exp(b_last) + _dot_tn(_pad_rows(v, LANES).astype(BF16), _pad_rows(kd, LANES).astype(BF16))
    return o, st_new


def _hgrn2_body(q_ref, f_ref, i_ref, g_ref, lb_ref, gn_ref, *refs, li, heads, chunk, has_state):
    if has_state:
        s0_ref, o_ref, so_ref, st_ref, cs_ref = refs
    else:
        o_ref, so_ref, st_ref, cs_ref = refs
    t = pl.program_id(2)
    tb = q_ref.shape[0]
    sub = min(LANES, tb)
    n_chunks = tb // chunk
    chunks_per_sub = sub // chunk

    @pl.when(t == 0)
    def _init():
        for h in range(heads):
            st_ref[h] = s0_ref[0, h].T if has_state else jnp.zeros((DV_B, DK_B), F32)

    for h in range(heads):
        hs = slice(h * DK_B, (h + 1) * DK_B)
        lbr = lb_ref[:, hs]
        e = jnp.exp(lbr - jnp.max(lbr, axis=0, keepdims=True))
        sm = e / jnp.sum(e, axis=0, keepdims=True)
        lb = jnp.sum(sm[:li + 1], axis=0, keepdims=True)
        for u in range(tb // sub):
            rs = slice(u * sub, (u + 1) * sub)
            cs_ref[rs, :] = _cumsum_rows(jnp.log(lb + (1.0 - lb) * _sigmoid(f_ref[rs, hs])))

        def chunk_step(ci, carry, hs=hs, lb=lb, h=h):
            r0 = pl.multiple_of(ci * chunk, chunk)
            rs = pl.ds(r0, chunk)
            base = cs_ref[pl.ds(jnp.maximum(r0 - 1, 0), 1), :]
            base = jnp.where(ci % chunks_per_sub == 0, 0.0, base)
            q_raw = q_ref[rs, hs]
            q = q_raw * _sigmoid(q_raw)
            fg = lb + (1.0 - lb) * _sigmoid(f_ref[rs, hs])
            o, st_new = _gla_chunk(q, 1.0 - fg, i_ref[rs, hs], cs_ref[rs, :] - base, st_ref[h])
            st_ref[h] = st_new
            g_raw = g_ref[rs, hs]
            o_ref[rs, hs] = _rms(o, gn_ref[...], NORM_EPS) * (g_raw * _sigmoid(g_raw))
            return carry

        lax.fori_loop(0, n_chunks, chunk_step, 0)

    @pl.when(t == pl.num_programs(2) - 1)
    def _fin():
        for h in range(heads):
            so_ref[0, h] = st_ref[h].T


def hgrn2(z, lb_raw, gnorm, li, *, n, t, row0, cols, state):
    chunk = min(32, t)
    tb = _tile(t, 256, chunk)
    heads = H_B if t < 64 else 1
    hw = heads * DK_B
    n_tb = t // tb
    assert row0 % tb == 0 and all(c % hw == 0 for c in cols)
    rb0 = row0 // tb
    n_lb = lb_raw.shape[0]

    def zspec(col):
        return pl.BlockSpec((tb, hw), lambda b, h, c: (rb0 + b * n_tb + c, col // hw + h))

    in_specs = [zspec(c) for c in cols] + [
        pl.BlockSpec((n_lb, hw), lambda b, h, c: (0, h)),
        pl.BlockSpec((1, DV_B), lambda b, h, c: (0, 0))]
    args = [z, z, z, z, lb_raw, gnorm.reshape(1, DV_B)]
    if state is not None:
        in_specs.append(pl.BlockSpec((1, heads, DK_B, DV_B), lambda b, h, c: (b, h, 0, 0)))
        args.append(state)
    return pl.pallas_call(
        functools.partial(_hgrn2_body, li=li, heads=heads, chunk=chunk, has_state=state is not None),
        grid=(n, H_B // heads, n_tb),
        in_specs=in_specs,
        out_specs=[pl.BlockSpec((tb, hw), lambda b, h, c: (b * n_tb + c, h)),
                   pl.BlockSpec((1, heads, DK_B, DV_B), lambda b, h, c: (b, h, 0, 0))],
        out_shape=[jax.ShapeDtypeStruct((n * t, H_B * DV_B), F32),
                   jax.ShapeDtypeStruct((n, H_B, DK_B, DV_B), F32)],
        scratch_shapes=[pltpu.VMEM((heads, DV_B, DK_B), F32), pltpu.VMEM((tb, DK_B), F32)],
        compiler_params=_cparams("parallel", "parallel", "arbitrary"),
        name="hgrn2",
    )(*args)


def _mla_prep_body(cq_ref, ckv_ref, kpe_ref, cosq_ref, sinq_ref, qn_ref, kvn_ref, wn_ref, wp_ref, wr_ref,
                   wuk_ref, qlat_ref, qpe_ref, lat_ref, kr_ref):
    qn = _rms(cq_ref[...], qn_ref[...], NORM_EPS).astype(BF16)
    q_nope = jnp.dot(qn, wn_ref[...], preferred_element_type=F32)
    q_pe = jnp.dot(qn, wp_ref[...], preferred_element_type=F32)
    q_rot = jnp.dot(qn, wr_ref[...], preferred_element_type=F32)
    cosq, sinq = cosq_ref[...], sinq_ref[...]
    qpe_ref[...] = q_pe * cosq + q_rot * sinq
    for h in range(H_C):
        qh = q_nope[:, h * DN_C:(h + 1) * DN_C].astype(BF16)
        qlat_ref[:, h * DC_KV:(h + 1) * DC_KV] = jnp.dot(qh, wuk_ref[h], preferred_element_type=F32)
    lat_ref[...] = _rms(ckv_ref[...], kvn_ref[...], NORM_EPS)
    kp = kpe_ref[...]
    kr_ref[...] = kp[:, :DR_C] * cosq[:, :DR_C] + kp[:, DR_C:] * sinq[:, :DR_C]


def mla_prep(z, cos_t, sin_t, q_norm, kv_norm, w_nope, w_pe, w_rot, w_uk, cq_col, ckv_col, kpe_col):
    m = z.shape[0]
    tm = _tile(m, 256, SUBLANES)
    full = lambda a: pl.BlockSpec(a.shape, lambda i: (0,) * a.ndim)
    qn2, kvn2 = q_norm.reshape(1, DC_Q), kv_norm.reshape(1, DC_KV)
    return pl.pallas_call(
        _mla_prep_body,
        grid=(m // tm,),
        in_specs=[pl.BlockSpec((tm, DC_Q), lambda i: (i, cq_col // DC_Q)),
                  pl.BlockSpec((tm, DC_KV), lambda i: (i, ckv_col // DC_KV)),
                  pl.BlockSpec((tm, 2 * DR_C), lambda i: (i, kpe_col // (2 * DR_C))),
                  pl.BlockSpec((tm, H_C * DR_C), lambda i: (i, 0)),
                  pl.BlockSpec((tm, H_C * DR_C), lambda i: (i, 0)),
                  full(qn2), full(kvn2), full(w_nope), full(w_pe), full(w_rot), full(w_uk)],
        out_specs=[pl.BlockSpec((tm, H_C * DC_KV), lambda i: (i, 0)),
                   pl.BlockSpec((tm, H_C * DR_C), lambda i: (i, 0)),
                   pl.BlockSpec((tm, DC_KV), lambda i: (i, 0)),
                   pl.BlockSpec((tm, DR_C), lambda i: (i, 0))],
        out_shape=[jax.ShapeDtypeStruct((m, H_C * DC_KV), F32),
                   jax.ShapeDtypeStruct((m, H_C * DR_C), F32),
                   jax.ShapeDtypeStruct((m, DC_KV), F32),
                   jax.ShapeDtypeStruct((m, DR_C), F32)],
        compiler_params=_cparams("parallel"),
        name="mla_prep",
    )(z, z, z, cos_t, sin_t, qn2, kvn2, w_nope, w_pe, w_rot, w_uk)


def _q_scratch_dtype(tq):
    return BF16 if tq % BF16_ROWS == 0 else F32


def _mla_stack_q(qlat_ref, qpe_ref, qsl_ref, qsp_ref):
    tq = qlat_ref.shape[0]
    for h in range(H_C):
        qsl_ref[h * tq:(h + 1) * tq, :] = qlat_ref[:, h * DC_KV:(h + 1) * DC_KV].astype(qsl_ref.dtype)
        qsp_ref[h * tq:(h + 1) * tq, :] = qpe_ref[:, h * DR_C:(h + 1) * DR_C].astype(qsp_ref.dtype)


def _mla_finish(acc_ref, l_ref, wuv_ref, o_ref):
    tq = o_ref.shape[0]
    o = acc_ref[...] / l_ref[...]
    for h in range(H_C):
        oh = _pad_rows(o[h * tq:(h + 1) * tq], BF16_ROWS).astype(BF16)
        o_ref[:, h * DV_C:(h + 1) * DV_C] = jnp.dot(oh, wuv_ref[h], preferred_element_type=F32)[:tq]


def _diff_stack_q(dq_ref, qs_ref):
    tq = dq_ref.shape[0]
    qs_ref[...] = jnp.zeros_like(qs_ref)
    for jx in range(2):
        for r in range(H_D):
            r0 = (jx * H_D + r) * tq
            c0 = r * 2 * DH_D + jx * DH_D
            qs_ref[r0:r0 + tq, jx * DH_D:(jx + 1) * DH_D] = dq_ref[:, c0:c0 + DH_D].astype(qs_ref.dtype)


def _diff_finish(acc_ref, l_ref, lq1_ref, lk1_ref, lq2_ref, lk2_ref, sub_ref, o_ref, lam_init):
    tq = o_ref.shape[0]
    half = H_D * tq
    lam = (jnp.exp(jnp.sum(lq1_ref[...] * lk1_ref[...], axis=-1, keepdims=True))
           - jnp.exp(jnp.sum(lq2_ref[...] * lk2_ref[...], axis=-1, keepdims=True)) + lam_init)
    o = acc_ref[...] / l_ref[...]
    o = o[:half] - lam * o[half:]
    o = _rms(o, sub_ref[...], SUBLN_EPS) * (1.0 - lam_init)
    _unstack_heads(o, o_ref, H_D, 2 * DH_D)


def _causal_ok(rows, tq, tk, q0, k0):
    tok = q0 + lax.broadcasted_iota(jnp.int32, (rows, tk), 0) % tq
    key = k0 + lax.broadcasted_iota(jnp.int32, (rows, tk), 1)
    return key <= tok


def _flash_init(m_ref, l_ref, acc_ref):
    m_ref[...] = jnp.full(m_ref.shape, NEG, F32)
    l_ref[...] = jnp.zeros_like(l_ref)
    acc_ref[...] = jnp.zeros_like(acc_ref)


def _mla_prompt_body(qlat_ref, qpe_ref, lat_ref, kr_ref, wuv_ref, o_ref, qsl_ref, qsp_ref, m_ref, l_ref, acc_ref):
    i, j = pl.program_id(1), pl.program_id(2)
    tq, tk = qlat_ref.shape[0], lat_ref.shape[0]
    scale = (DN_C + DR_C) ** -0.5

    @pl.when(j == 0)
    def _init():
        _mla_stack_q(qlat_ref, qpe_ref, qsl_ref, qsp_ref)
        _flash_init(m_ref, l_ref, acc_ref)

    @pl.when(j * tk <= i * tq + tq - 1)
    def _step():
        lat = lat_ref[...].astype(BF16)
        s = (_dot_nt(qsl_ref[...].astype(BF16), lat)
             + _dot_nt(qsp_ref[...].astype(BF16), kr_ref[...].astype(BF16))) * scale
        s = jnp.where(_causal_ok(H_C * tq, tq, tk, i * tq, j * tk), s, NEG)
        alpha, p = _online_softmax(s, m_ref, l_ref)
        acc_ref[...] = alpha * acc_ref[...] + jnp.dot(p.astype(BF16), lat, preferred_element_type=F32)

    @pl.when(j == pl.num_programs(2) - 1)
    def _fin():
        _mla_finish(acc_ref, l_ref, wuv_ref, o_ref)


def mla_prompt(q_lat, q_pe, lat, kr, w_uv, batch, seq):
    tq = _tile(seq, 128, BF16_ROWS)
    tk = _tile(seq, 512, LANES)
    nq, nk = seq // tq, seq // tk
    rows = H_C * tq
    kv_idx = lambda b, i, j: (b * nk + jnp.minimum(j, (i * tq + tq - 1) // tk), 0)
    qdt = _q_scratch_dtype(tq)
    return pl.pallas_call(
        _mla_prompt_body,
        grid=(batch, nq, nk),
        in_specs=[pl.BlockSpec((tq, H_C * DC_KV), lambda b, i, j: (b * nq + i, 0)),
                  pl.BlockSpec((tq, H_C * DR_C), lambda b, i, j: (b * nq + i, 0)),
                  pl.BlockSpec((tk, DC_KV), kv_idx),
                  pl.BlockSpec((tk, DR_C), kv_idx),
                  pl.BlockSpec(w_uv.shape, lambda b, i, j: (0, 0, 0))],
        out_specs=pl.BlockSpec((tq, H_C * DV_C), lambda b, i, j: (b * nq + i, 0)),
        out_shape=jax.ShapeDtypeStruct((batch * seq, H_C * DV_C), F32),
        scratch_shapes=[pltpu.VMEM((rows, DC_KV), qdt), pltpu.VMEM((rows, DR_C), qdt),
                        pltpu.VMEM((rows, 1), F32), pltpu.VMEM((rows, 1), F32),
                        pltpu.VMEM((rows, DC_KV), F32)],
        compiler_params=_cparams("parallel", "parallel", "arbitrary"),
        name="mla_prompt",
    )(q_lat, q_pe, lat, kr, w_uv)


def _diff_prompt_body(dq_ref, dk_ref, dv_ref, lq1_ref, lk1_ref, lq2_ref, lk2_ref, sub_ref, o_ref,
                      qs_ref, m_ref, l_ref, acc_ref, *, lam_init):
    i, j = pl.program_id(1), pl.program_id(2)
    tq, tk = dq_ref.shape[0], dk_ref.shape[0]
    scale = DH_D ** -0.5

    @pl.when(j == 0)
    def _init():
        _diff_stack_q(dq_ref, qs_ref)
        _flash_init(m_ref, l_ref, acc_ref)

    @pl.when(j * tk <= i * tq + tq - 1)
    def _step():
        s = _dot_nt(qs_ref[...].astype(BF16), dk_ref[...].astype(BF16)) * scale
        s = jnp.where(_causal_ok(2 * H_D * tq, tq, tk, i * tq, j * tk), s, NEG)
        alpha, p = _online_softmax(s, m_ref, l_ref)
        acc_ref[...] = alpha * acc_ref[...] + jnp.dot(p.astype(BF16), dv_ref[...].astype(BF16),
                                                      preferred_element_type=F32)

    @pl.when(j == pl.num_programs(2) - 1)
    def _fin():
        _diff_finish(acc_ref, l_ref, lq1_ref, lk1_ref, lq2_ref, lk2_ref, sub_ref, o_ref, lam_init)


def diff_prompt(z, lams, subln, lam_init, batch, seq, dq_col, dk_col, dv_col):
    tq = _tile(seq, 256, BF16_ROWS)
    tk = _tile(seq, 512, LANES)
    nq, nk = seq // tq, seq // tk
    rows = 2 * H_D * tq
    wq, wkv = H_D * 2 * DH_D, 2 * DH_D
    kv_blk = lambda b, i, j: b * nk + jnp.minimum(j, (i * tq + tq - 1) // tk)
    vec = lambda a: pl.BlockSpec(a.shape, lambda b, i, j: (0, 0))
    return pl.pallas_call(
        functools.partial(_diff_prompt_body, lam_init=lam_init),
        grid=(batch, nq, nk),
        in_specs=[pl.BlockSpec((tq, wq), lambda b, i, j: (b * nq + i, dq_col // wq)),
                  pl.BlockSpec((tk, wkv), lambda b, i, j: (kv_blk(b, i, j), dk_col // wkv)),
                  pl.BlockSpec((tk, wkv), lambda b, i, j: (kv_blk(b, i, j), dv_col // wkv))]
                 + [vec(a) for a in lams] + [vec(subln)],
        out_specs=pl.BlockSpec((tq, wq), lambda b, i, j: (b * nq + i, 0)),
        out_shape=jax.ShapeDtypeStruct((batch * seq, wq), F32),
        scratch_shapes=[pltpu.VMEM((rows, wkv), _q_scratch_dtype(tq)), pltpu.VMEM((rows, 1), F32),
                        pltpu.VMEM((rows, 1), F32), pltpu.VMEM((rows, wkv), F32)],
        compiler_params=_cparams("parallel", "parallel", "arbitrary"),
        name="diff_prompt",
    )(z, z, z, *lams, subln)


def _new_token_init(s_new, v_new, m_ref, l_ref, acc_ref, dec_seq):
    s_new = _new_token_scores(s_new, dec_seq)
    m = jnp.max(s_new, axis=1, keepdims=True)
    p = jnp.exp(s_new - m)
    m_ref[...] = m
    l_ref[...] = jnp.sum(p, axis=1, keepdims=True)
    acc_ref[...] = _new_token_values(p, v_new)


def _mla_sample_body(pt_ref, qlat_ref, qpe_ref, nlat_ref, nkr_ref, wuv_ref, *refs, pages_per_step):
    del pt_ref
    pp = pages_per_step
    lat_refs, kr_refs = refs[:pp], refs[pp:2 * pp]
    o_ref, qsl_ref, qsp_ref, m_ref, l_ref, acc_ref = refs[2 * pp:]
    j = pl.program_id(1)
    dec_seq = qlat_ref.shape[0]
    scale = (DN_C + DR_C) ** -0.5

    @pl.when(j == 0)
    def _init():
        _mla_stack_q(qlat_ref, qpe_ref, qsl_ref, qsp_ref)
        nlat = nlat_ref[...]
        s_new = (_dot_nt(qsl_ref[...].astype(BF16), _pad_rows(nlat, LANES).astype(BF16))
                 + _dot_nt(qsp_ref[...].astype(BF16), _pad_rows(nkr_ref[...], LANES).astype(BF16))) * scale
        _new_token_init(s_new, nlat, m_ref, l_ref, acc_ref, dec_seq)

    ql, qp = qsl_ref[...].astype(BF16), qsp_ref[...].astype(BF16)
    lats = [r[...].astype(BF16) for r in lat_refs]
    s = jnp.concatenate([_dot_nt(ql, lats[u]) + _dot_nt(qp, kr_refs[u][...].astype(BF16))
                         for u in range(pp)], axis=1) * scale
    alpha, p = _online_softmax(s, m_ref, l_ref)
    acc = alpha * acc_ref[...]
    for u in range(pp):
        acc = acc + jnp.dot(p[:, u * PAGE_SIZE:(u + 1) * PAGE_SIZE].astype(BF16), lats[u],
                            preferred_element_type=F32)
    acc_ref[...] = acc

    @pl.when(j == pl.num_programs(1) - 1)
    def _fin():
        _mla_finish(acc_ref, l_ref, wuv_ref, o_ref)


def _page_specs(shape_tail, li, pp):
    def idx(u):
        return lambda s, j, pt: (li, pt[s, j * pp + u]) + (0,) * len(shape_tail)
    return [pl.BlockSpec((None, None) + shape_tail, idx(u)) for u in range(pp)]


def mla_sample(q_lat, q_pe, lat, kr, w_uv, cache_lat, cache_kr, li, page_table, row0, dec_seq):
    n_seq, n_pages = page_table.shape
    pp = _tile(n_pages, 16, 1)
    rb0 = row0 // dec_seq
    rows = H_C * dec_seq
    qrow = lambda s, j, pt: (rb0 + s, 0)
    qdt = _q_scratch_dtype(dec_seq)
    grid_spec = pltpu.PrefetchScalarGridSpec(
        num_scalar_prefetch=1,
        grid=(n_seq, n_pages // pp),
        in_specs=[pl.BlockSpec((dec_seq, H_C * DC_KV), qrow),
                  pl.BlockSpec((dec_seq, H_C * DR_C), qrow),
                  pl.BlockSpec((dec_seq, DC_KV), qrow),
                  pl.BlockSpec((dec_seq, DR_C), qrow),
                  pl.BlockSpec(w_uv.shape, lambda s, j, pt: (0, 0, 0))]
                 + _page_specs((PAGE_SIZE, DC_KV), li, pp) + _page_specs((PAGE_SIZE, DR_C), li, pp),
        out_specs=pl.BlockSpec((dec_seq, H_C * DV_C), lambda s, j, pt: (s, 0)),
        scratch_shapes=[pltpu.VMEM((rows, DC_KV), qdt), pltpu.VMEM((rows, DR_C), qdt),
                        pltpu.VMEM((rows, 1), F32), pltpu.VMEM((rows, 1), F32),
                        pltpu.VMEM((rows, DC_KV), F32)],
    )
    return pl.pallas_call(
        functools.partial(_mla_sample_body, pages_per_step=pp),
        grid_spec=grid_spec,
        out_shape=jax.ShapeDtypeStruct((n_seq * dec_seq, H_C * DV_C), F32),
        compiler_params=_cparams("parallel", "arbitrary"),
        name="mla_sample",
    )(page_table, q_lat, q_pe, lat, kr, w_uv, *([cache_lat] * pp), *([cache_kr] * pp))


def _diff_sample_body(pt_ref, dq_ref, nk_ref, nv_ref, lq1_ref, lk1_ref, lq2_ref, lk2_ref, sub_ref, *refs,
                      pages_per_step, lam_init):
    del pt_ref
    pp = pages_per_step
    k_refs, v_refs = refs[:pp], refs[pp:2 * pp]
    o_ref, qs_ref, m_ref, l_ref, acc_ref = refs[2 * pp:]
    j = pl.program_id(1)
    dec_seq = dq_ref.shape[0]
    scale = DH_D ** -0.5

    @pl.when(j == 0)
    def _init():
        _diff_stack_q(dq_ref, qs_ref)
        s_new = _dot_nt(qs_ref[...].astype(BF16), _pad_rows(nk_ref[...], LANES).astype(BF16)) * scale
        _new_token_init(s_new, nv_ref[...], m_ref, l_ref, acc_ref, dec_seq)

    q = qs_ref[...].astype(BF16)
    s = jnp.concatenate([_dot_nt(q, k_refs[u][...].astype(BF16)) for u in range(pp)], axis=1) * scale
    alpha, p = _online_softmax(s, m_ref, l_ref)
    acc = alpha * acc_ref[...]
    for u in range(pp):
        acc = acc + jnp.dot(p[:, u * PAGE_SIZE:(u + 1) * PAGE_SIZE].astype(BF16), v_refs[u][...].astype(BF16),
                            preferred_element_type=F32)
    acc_ref[...] = acc

    @pl.when(j == pl.num_programs(1) - 1)
    def _fin():
        _diff_finish(acc_ref, l_ref, lq1_ref, lk1_ref, lq2_ref, lk2_ref, sub_ref, o_ref, lam_init)


def diff_sample(z, lams, subln, lam_init, cache_k, cache_v, li, page_table, row0, dec_seq, dq_col, dk_col, dv_col):
    n_seq, n_pages = page_table.shape
    pp = _tile(n_pages, 16, 1)
    rb0 = row0 // dec_seq
    rows = 2 * H_D * dec_seq
    wq, wkv = H_D * 2 * DH_D, 2 * DH_D
    ck = cache_k.reshape(cache_k.shape[0], cache_k.shape[1], PAGE_SIZE, wkv)
    cv = cache_v.reshape(cache_v.shape[0], cache_v.shape[1], PAGE_SIZE, wkv)
    vec = lambda a: pl.BlockSpec(a.shape, lambda s, j, pt: (0, 0))
    grid_spec = pltpu.PrefetchScalarGridSpec(
        num_scalar_prefetch=1,
        grid=(n_seq, n_pages // pp),
        in_specs=[pl.BlockSpec((dec_seq, wq), lambda s, j, pt: (rb0 + s, dq_col // wq)),
                  pl.BlockSpec((dec_seq, wkv), lambda s, j, pt: (rb0 + s, dk_col // wkv)),
                  pl.BlockSpec((dec_seq, wkv), lambda s, j, pt: (rb0 + s, dv_col // wkv))]
                 + [vec(a) for a in lams] + [vec(subln)]
                 + _page_specs((PAGE_SIZE, wkv), li, pp) + _page_specs((PAGE_SIZE, wkv), li, pp),
        out_specs=pl.BlockSpec((dec_seq, wq), lambda s, j, pt: (s, 0)),
        scratch_shapes=[pltpu.VMEM((rows, wkv), _q_scratch_dtype(dec_seq)), pltpu.VMEM((rows, 1), F32),
                        pltpu.VMEM((rows, 1), F32), pltpu.VMEM((rows, wkv), F32)],
    )
    return pl.pallas_call(
        functools.partial(_diff_sample_body, pages_per_step=pp, lam_init=lam_init),
        grid_spec=grid_spec,
        out_shape=jax.ShapeDtypeStruct((n_seq * dec_seq, wq), F32),
        compiler_params=_cparams("parallel", "arbitrary"),
        name="diff_sample",
    )(page_table, z, z, z, *lams, subln, *([ck] * pp), *([cv] * pp))


def _even_layer(x, li, w_in, w_out, lb_raw, gnorm, g_mix, cache_k, cache_v, state, page_table, dims):
    batch, seq, n_seq, dec_seq = dims
    n_p = batch * seq
    wa, wb = H_A * DH_A, H_B * DK_B
    n_a = wa + 2 * DH_A
    w_cat = jnp.concatenate([w_in[:, n_a:], w_in[:, :n_a]], axis=1).astype(BF16)
    b_cols = tuple(u * wb for u in range(4))
    q_off = 4 * wb
    k_off, v_off = q_off + wa, q_off + wa + DH_A
    z = norm_matmul(x, g_mix, w_cat)
    means = moba_means(z, batch, seq, k_off)
    oa_p = moba_prompt(z, means, batch, seq, q_off, k_off, v_off)
    oa_s = moba_sample(z, cache_k, cache_v, li, page_table, n_p, dec_seq, q_off, k_off, v_off)
    ob_p, st_p = hgrn2(z, lb_raw, gnorm, li, n=batch, t=seq, row0=0, cols=b_cols, state=None)
    ob_s, st_s = hgrn2(z, lb_raw, gnorm, li, n=n_seq, t=dec_seq, row0=n_p, cols=b_cols, state=state[li])
    oa = jnp.concatenate([oa_p, oa_s], axis=0)
    ob = jnp.concatenate([ob_p, ob_s], axis=0)
    x = proj_residual(x, oa, ob, w_out.astype(BF16))
    ak, av = z[:, k_off:k_off + DH_A], z[:, v_off:v_off + DH_A]
    return x, (ak, av, st_p, st_s)


def _rope_tables(pos):
    half = DR_C // 2
    inv = ROPE_THETA ** (-jnp.arange(half, dtype=F32) / half)
    ang = pos.astype(F32)[:, None] * inv[None, :]
    cos = jnp.tile(jnp.cos(ang), (1, 2 * H_C))
    sin = jnp.tile(jnp.sin(ang), (1, 2 * H_C))
    return cos, sin


def _rot_cols(w):
    half = DR_C // 2
    return jnp.concatenate([-w[..., half:], w[..., :half]], axis=-1)


def _odd_layer(x, l, li, w_in, w_out, q_norm, kv_norm, w_uq, w_uk, w_uv, lams, subln, g_mix,
               cache_cl, cache_cr, cache_dk, cache_dv, page_table, pos, dims):
    batch, seq, n_seq, dec_seq = dims
    n_p = batch * seq
    lam_init = 0.8 - 0.6 * math.exp(-0.3 * l)
    wq, wkv = H_D * 2 * DH_D, 2 * DH_D
    o_kpe = DC_Q + DC_KV
    o_dq = o_kpe + DR_C
    w_kpe = w_in[:, o_kpe:o_dq]
    w_cat = jnp.concatenate([w_in[:, :o_kpe], w_in[:, o_dq:], w_kpe, _rot_cols(w_kpe)], axis=1).astype(BF16)
    n_cq, n_ckv, n_dq = 0, DC_Q, DC_Q + DC_KV
    n_dk, n_dv, n_kpe = n_dq + wq, n_dq + wq + wkv, n_dq + wq + 2 * wkv
    z = norm_matmul(x, g_mix, w_cat)

    w3 = w_uq.reshape(DC_Q, H_C, DN_C + DR_C)
    w_nope = w3[:, :, :DN_C].reshape(DC_Q, H_C * DN_C).astype(BF16)
    w_pe3 = w3[:, :, DN_C:]
    w_pe = w_pe3.reshape(DC_Q, H_C * DR_C).astype(BF16)
    w_rot = _rot_cols(w_pe3).reshape(DC_Q, H_C * DR_C).astype(BF16)
    w_uk_t = jnp.transpose(w_uk, (1, 2, 0)).astype(BF16)
    w_uv_t = jnp.transpose(w_uv, (1, 0, 2)).astype(BF16)
    cos_t, sin_t = _rope_tables(pos)
    q_lat, q_pe, lat, kr = mla_prep(z, cos_t, sin_t, q_norm, kv_norm, w_nope, w_pe, w_rot, w_uk_t,
                                    n_cq, n_ckv, n_kpe)
    oc_p = mla_prompt(q_lat, q_pe, lat, kr, w_uv_t, batch, seq)
    oc_s = mla_sample(q_lat, q_pe, lat, kr, w_uv_t, cache_cl, cache_cr, li, page_table, n_p, dec_seq)
    lams2 = tuple(a.reshape(1, DH_D) for a in lams)
    sub2 = subln.reshape(1, 2 * DH_D)
    od_p = diff_prompt(z, lams2, sub2, lam_init, batch, seq, n_dq, n_dk, n_dv)
    od_s = diff_sample(z, lams2, sub2, lam_init, cache_dk, cache_dv, li, page_table, n_p, dec_seq,
                       n_dq, n_dk, n_dv)
    oc = jnp.concatenate([oc_p, oc_s], axis=0)
    od = jnp.concatenate([od_p, od_s], axis=0)
    x = proj_residual(x, oc, od, w_out.astype(BF16))
    return x, (lat, kr, z[:, n_dk:n_dk + wkv], z[:, n_dv:n_dv + wkv])


def kernel(x_prompt, x_sample, cache_a_k, cache_a_v, state_b, cache_c_latent, cache_c_krope, cache_d_k, cache_d_v,
           page_table, norm_mix, norm_ffn, norm_final, w_in_even, w_out_even, hgrn_lb, hgrn_gnorm, w_in_odd,
           w_out_odd, mla_q_norm, mla_kv_norm, w_mla_uq, w_mla_uk, w_mla_uv, diff_lq1, diff_lk1, diff_lq2,
           diff_lk2, diff_subln, w_ffn_gate, w_ffn_up, w_ffn_down):
    batch, seq, d = x_prompt.shape
    n_seq, dec_seq, _ = x_sample.shape
    depth = norm_mix.shape[0]
    n_p = batch * seq
    past_len = page_table.shape[1] * PAGE_SIZE
    dims = (batch, seq, n_seq, dec_seq)
    pos = jnp.concatenate([jnp.tile(jnp.arange(seq, dtype=jnp.int32), batch),
                           jnp.tile(past_len + jnp.arange(dec_seq, dtype=jnp.int32), n_seq)])
    x = jnp.concatenate([x_prompt.reshape(n_p, d), x_sample.reshape(n_seq * dec_seq, d)], axis=0)
    even, odd = [], []
    for l in range(depth):
        li = l // 2
        if l % 2 == 0:
            x, st = _even_layer(x, li, w_in_even[li], w_out_even[li], hgrn_lb, hgrn_gnorm[li], norm_mix[l],
                                cache_a_k, cache_a_v, state_b, page_table, dims)
            even.append(st)
        else:
            lams = (diff_lq1[li], diff_lk1[li], diff_lq2[li], diff_lk2[li])
            x, st = _odd_layer(x, l, li, w_in_odd[li], w_out_odd[li], mla_q_norm[li], mla_kv_norm[li],
                               w_mla_uq[li], w_mla_uk[li], w_mla_uv[li], lams, diff_subln[li], norm_mix[l],
                               cache_c_latent, cache_c_krope, cache_d_k, cache_d_v, page_table, pos, dims)
            odd.append(st)
        x = ffn(x, norm_ffn[l], norm_final, w_ffn_gate[l].astype(BF16), w_ffn_up[l].astype(BF16),
                w_ffn_down[l].astype(BF16), final_norm=(l == depth - 1))

    def split(a, tail):
        return a[:n_p].reshape((batch, seq) + tail), a[n_p:].reshape((n_seq, dec_seq) + tail)

    stack = lambda xs: jnp.stack(xs, axis=0)
    y_p, y_s = split(x, (d,))
    ak = [split(e[0], (1, DH_A)) for e in even]
    av = [split(e[1], (1, DH_A)) for e in even]
    cl = [split(o[0], (DC_KV,)) for o in odd]
    cr = [split(o[1], (DR_C,)) for o in odd]
    dk = [split(o[2], (1, 2 * DH_D)) for o in odd]
    dv = [split(o[3], (1, 2 * DH_D)) for o in odd]
    pick = lambda pairs, u: stack([p[u] for p in pairs])
    return (y_p, y_s,
            pick(ak, 0), pick(av, 0), stack([e[2] for e in even]),
            pick(cl, 0), pick(cr, 0), pick(dk, 0), pick(dv, 0),
            pick(ak, 1), pick(av, 1), stack([e[3] for e in even]),
            pick(cl, 1), pick(cr, 1), pick(dk, 1), pick(dv, 1))
```

```python
import functools
import math

import jax
import jax.numpy as jnp
from jax import lax
from jax.experimental import pallas as pl
from jax.experimental.pallas import tpu as pltpu

F32 = jnp.float32
BF16 = jnp.bfloat16
HIGHEST = lax.Precision.HIGHEST

PAGE_SIZE = 128
H_A, DH_A = 8, 128
MOBA_BLOCK, MOBA_TOPK = 256, 3
H_B, DK_B, DV_B = 8, 128, 128
H_C, DC_Q, DC_KV, DN_C, DR_C, DV_C = 8, 512, 512, 128, 64, 128
ROPE_THETA = 10000.0
H_D, DH_D = 4, 128
NORM_EPS = 1e-6
SUBLN_EPS = 1e-5
NEG = -1e30
BELOW_NEG = -3e38

VMEM_LIMIT_BYTES = 56 * 1024 * 1024
SUBLANES = 8
LANES = 128
BF16_ROWS = 16


def _cparams(*sem):
    return pltpu.CompilerParams(dimension_semantics=sem, vmem_limit_bytes=VMEM_LIMIT_BYTES)


def _tile(n, pref, mult):
    if n <= pref:
        return n
    t = (pref // mult) * mult
    while t >= mult:
        if n % t == 0:
            return t
        t -= mult
    raise ValueError(f"no tile for {n} (pref {pref}, mult {mult})")


def _dot_nt(a, b, precision=None):
    return lax.dot_general(a, b, (((1,), (1,)), ((), ())), preferred_element_type=F32, precision=precision)


def _dot_tn(a, b, precision=None):
    return lax.dot_general(a, b, (((0,), (0,)), ((), ())), preferred_element_type=F32, precision=precision)


def _sigmoid(x):
    return 1.0 / (1.0 + jnp.exp(-x))


def _rms(x, g, eps):
    return x * lax.rsqrt(jnp.mean(x * x, axis=-1, keepdims=True) + eps) * g


def _pad_rows(x, n):
    if x.shape[0] >= n:
        return x
    return jnp.concatenate([x, jnp.zeros((n - x.shape[0], x.shape[1]), x.dtype)], axis=0)


def _online_softmax(s, m_ref, l_ref):
    m_prev = m_ref[...]
    m_new = jnp.maximum(m_prev, jnp.max(s, axis=1, keepdims=True))
    alpha = jnp.exp(m_prev - m_new)
    p = jnp.exp(s - m_new)
    l_ref[...] = alpha * l_ref[...] + jnp.sum(p, axis=1, keepdims=True)
    m_ref[...] = m_new
    return alpha, p


def _norm_matmul_body(x_ref, g_ref, w_ref, o_ref, h_ref):
    @pl.when(pl.program_id(1) == 0)
    def _():
        h_ref[...] = _rms(x_ref[...], g_ref[...], NORM_EPS).astype(BF16)

    o_ref[...] = jnp.dot(h_ref[...], w_ref[...], preferred_element_type=F32)


def norm_matmul(x, g, w):
    m, d = x.shape
    n = w.shape[1]
    tm = _tile(m, 512, SUBLANES)
    tn = _tile(n, 896, LANES)
    return pl.pallas_call(
        _norm_matmul_body,
        grid=(m // tm, n // tn),
        in_specs=[pl.BlockSpec((tm, d), lambda i, j: (i, 0)),
                  pl.BlockSpec((1, d), lambda i, j: (0, 0)),
                  pl.BlockSpec((d, tn), lambda i, j: (0, j))],
        out_specs=pl.BlockSpec((tm, tn), lambda i, j: (i, j)),
        out_shape=jax.ShapeDtypeStruct((m, n), F32),
        scratch_shapes=[pltpu.VMEM((tm, d), BF16)],
        compiler_params=_cparams("parallel", "arbitrary"),
        name="norm_matmul",
    )(x, g.reshape(1, d), w)


def _proj_residual_body(x_ref, a1_ref, a2_ref, w1_ref, w2_ref, o_ref):
    y = jnp.dot(a1_ref[...].astype(BF16), w1_ref[...], preferred_element_type=F32)
    y = y + jnp.dot(a2_ref[...].astype(BF16), w2_ref[...], preferred_element_type=F32)
    o_ref[...] = x_ref[...] + y


def proj_residual(x, a1, a2, w):
    m, d = x.shape
    k1, k2 = a1.shape[1], a2.shape[1]
    assert w.shape == (k1 + k2, d) and k1 == k2
    tm = _tile(m, 512, SUBLANES)
    tn = _tile(d, 1024, LANES)
    return pl.pallas_call(
        _proj_residual_body,
        grid=(m // tm, d // tn),
        in_specs=[pl.BlockSpec((tm, tn), lambda i, j: (i, j)),
                  pl.BlockSpec((tm, k1), lambda i, j: (i, 0)),
                  pl.BlockSpec((tm, k2), lambda i, j: (i, 0)),
                  pl.BlockSpec((k1, tn), lambda i, j: (0, j)),
                  pl.BlockSpec((k2, tn), lambda i, j: (1, j))],
        out_specs=pl.BlockSpec((tm, tn), lambda i, j: (i, j)),
        out_shape=jax.ShapeDtypeStruct((m, d), F32),
        compiler_params=_cparams("parallel", "arbitrary"),
        name="proj_residual",
    )(x, a1, a2, w, w)


def _ffn_body(x_ref, g_ref, gf_ref, wg_ref, wu_ref, wd_ref, o_ref, h_ref, acc_ref, *, final_norm):
    j = pl.program_id(1)

    @pl.when(j == 0)
    def _():
        h_ref[...] = _rms(x_ref[...], g_ref[...], NORM_EPS).astype(BF16)
        acc_ref[...] = jnp.zeros_like(acc_ref)

    h = h_ref[...]
    gate = jnp.dot(h, wg_ref[...], preferred_element_type=F32)
    up = jnp.dot(h, wu_ref[...], preferred_element_type=F32)
    act = (gate * _sigmoid(gate) * up).astype(BF16)
    acc_ref[...] += jnp.dot(act, wd_ref[...], preferred_element_type=F32)

    @pl.when(j == pl.num_programs(1) - 1)
    def _():
        y = x_ref[...] + acc_ref[...]
        if final_norm:
            y = _rms(y, gf_ref[...], NORM_EPS)
        o_ref[...] = y


def ffn(x, g, gf, wg, wu, wd, *, final_norm):
    m, d = x.shape
    f = wg.shape[1]
    tm = _tile(m, 512, SUBLANES)
    tf = _tile(f, 512, LANES)
    return pl.pallas_call(
        functools.partial(_ffn_body, final_norm=final_norm),
        grid=(m // tm, f // tf),
        in_specs=[pl.BlockSpec((tm, d), lambda i, j: (i, 0)),
                  pl.BlockSpec((1, d), lambda i, j: (0, 0)),
                  pl.BlockSpec((1, d), lambda i, j: (0, 0)),
                  pl.BlockSpec((d, tf), lambda i, j: (0, j)),
                  pl.BlockSpec((d, tf), lambda i, j: (0, j)),
                  pl.BlockSpec((tf, d), lambda i, j: (j, 0))],
        out_specs=pl.BlockSpec((tm, d), lambda i, j: (i, 0)),
        out_shape=jax.ShapeDtypeStruct((m, d), F32),
        scratch_shapes=[pltpu.VMEM((tm, d), BF16), pltpu.VMEM((tm, d), F32)],
        compiler_params=_cparams("parallel", "arbitrary"),
        name="ffn",
    )(x, g.reshape(1, d), gf.reshape(1, d), wg, wu, wd)


def _moba_means_body(k_ref, o_ref):
    t, dh = k_ref.shape
    k = k_ref[...].reshape(t // MOBA_BLOCK, MOBA_BLOCK, dh)
    o_ref[0] = jnp.sum(k, axis=1) * (1.0 / MOBA_BLOCK)


def moba_means(z, batch, seq, k_col):
    nb = seq // MOBA_BLOCK
    return pl.pallas_call(
        _moba_means_body,
        grid=(batch,),
        in_specs=[pl.BlockSpec((seq, DH_A), lambda b: (b, k_col // DH_A))],
        out_specs=pl.BlockSpec((1, nb, DH_A), lambda b: (b, 0, 0)),
        out_shape=jax.ShapeDtypeStruct((batch, nb, DH_A), F32),
        compiler_params=_cparams("parallel"),
        name="moba_means",
    )(z)


def _top_blocks(gate, n_valid):
    nb = gate.shape[1]
    col = lax.broadcasted_iota(jnp.int32, gate.shape, 1)
    colf = col.astype(F32)
    valid = col < n_valid
    g = jnp.where(valid, gate, NEG)
    sel = jnp.zeros(gate.shape, F32)
    for _ in range(MOBA_TOPK):
        mx = jnp.max(g, axis=1, keepdims=True)
        idx = jnp.min(jnp.where(g == mx, colf, float(nb)), axis=1, keepdims=True)
        pick = colf == idx
        sel = jnp.where(jnp.logical_and(pick, valid), 1.0, sel)
        g = jnp.where(pick, BELOW_NEG, g)
    return sel


def _stack_heads(q_ref, n_heads, width):
    return jnp.concatenate([q_ref[:, h * width:(h + 1) * width] for h in range(n_heads)], axis=0)


def _unstack_heads(o, o_ref, n_heads, width):
    tq = o_ref.shape[0]
    for h in range(n_heads):
        o_ref[:, h * width:(h + 1) * width] = o[h * tq:(h + 1) * tq]


def _moba_prompt_body(q_ref, k_ref, v_ref, mean_ref, o_ref, qs_ref, sel_ref, m_ref, l_ref, acc_ref):
    i = pl.program_id(1)
    jj = pl.program_id(2)
    tq = MOBA_BLOCK
    scale = DH_A ** -0.5

    @pl.when(jj == 0)
    def _init():
        qf = _stack_heads(q_ref, H_A, DH_A)
        qs_ref[...] = qf.astype(BF16)
        gate = _dot_nt(qf, mean_ref[0], precision=HIGHEST)
        sel_ref[...] = _top_blocks(gate, i)
        m_ref[...] = jnp.full(m_ref.shape, NEG, F32)
        l_ref[...] = jnp.zeros_like(l_ref)
        acc_ref[...] = jnp.zeros_like(acc_ref)

    def step(ok):
        s = _dot_nt(qs_ref[...], k_ref[...].astype(BF16)) * scale
        s = jnp.where(ok, s, NEG)
        alpha, p = _online_softmax(s, m_ref, l_ref)
        acc_ref[...] = alpha * acc_ref[...] + jnp.dot(p.astype(BF16), v_ref[...].astype(BF16),
                                                      preferred_element_type=F32)

    @pl.when(jj == 0)
    def _own():
        shape = (H_A * tq, tq)
        tok = lax.broadcasted_iota(jnp.int32, shape, 0) % tq
        key = lax.broadcasted_iota(jnp.int32, shape, 1)
        step(key <= tok)

    @pl.when(jnp.logical_and(jj > 0, jj <= i))
    def _past():
        n = i - jj
        blk = lax.broadcasted_iota(jnp.int32, sel_ref.shape, 1)
        picked = jnp.sum(jnp.where(blk == n, sel_ref[...], 0.0), axis=1, keepdims=True)
        step(picked > 0.5)

    @pl.when(jj == pl.num_programs(2) - 1)
    def _fin():
        _unstack_heads(acc_ref[...] / l_ref[...], o_ref, H_A, DH_A)


def moba_prompt(z, means, batch, seq, q_col, k_col, v_col):
    nb = seq // MOBA_BLOCK
    rows = H_A * MOBA_BLOCK
    wq = H_A * DH_A
    kv_row = lambda b, i, jj: b * nb + jnp.maximum(i - jj, 0)
    return pl.pallas_call(
        _moba_prompt_body,
        grid=(batch, nb, nb),
        in_specs=[pl.BlockSpec((MOBA_BLOCK, wq), lambda b, i, jj: (b * nb + i, q_col // wq)),
                  pl.BlockSpec((MOBA_BLOCK, DH_A), lambda b, i, jj: (kv_row(b, i, jj), k_col // DH_A)),
                  pl.BlockSpec((MOBA_BLOCK, DH_A), lambda b, i, jj: (kv_row(b, i, jj), v_col // DH_A)),
                  pl.BlockSpec((1, nb, DH_A), lambda b, i, jj: (b, 0, 0))],
        out_specs=pl.BlockSpec((MOBA_BLOCK, wq), lambda b, i, jj: (b * nb + i, 0)),
        out_shape=jax.ShapeDtypeStruct((batch * seq, wq), F32),
        scratch_shapes=[pltpu.VMEM((rows, DH_A), BF16), pltpu.VMEM((rows, nb), F32),
                        pltpu.VMEM((rows, 1), F32), pltpu.VMEM((rows, 1), F32),
                        pltpu.VMEM((rows, DH_A), F32)],
        compiler_params=_cparams("parallel", "parallel", "arbitrary"),
        name="moba_prompt",
    )(z, z, z, means)


def _new_token_scores(s_new, dec_seq):
    rows = s_new.shape[0]
    tok = lax.broadcasted_iota(jnp.int32, (rows, LANES), 0) % dec_seq
    key = lax.broadcasted_iota(jnp.int32, (rows, LANES), 1)
    return jnp.where(key <= tok, s_new, NEG)


def _new_token_values(p_new, v_new):
    acc = p_new[:, 0:1] * v_new[0:1]
    for c in range(1, v_new.shape[0]):
        acc = acc + p_new[:, c:c + 1] * v_new[c:c + 1]
    return acc


def _page_dma(action, pt_ref, seq, first_page, slot, pp, page0, streams):
    for u in range(pp):
        page = page0 + pt_ref[seq, first_page + u]
        for hbm, rows, buf, sem in streams:
            r0 = pl.multiple_of(page * rows, rows)
            copy = pltpu.make_async_copy(hbm.at[pl.ds(r0, rows)], buf.at[slot, u], sem.at[slot])
            if action == "start":
                copy.start()
            else:
                copy.wait()


def _paged_prefetch(dma, steps_per_seq):
    seq, step = pl.program_id(0), pl.program_id(1)
    g = seq * steps_per_seq + step
    slot = g % 2

    @pl.when(g == 0)
    def _first():
        dma("start", seq, step, slot)

    @pl.when(g + 1 < pl.num_programs(0) * steps_per_seq)
    def _next():
        wrap = step + 1 == steps_per_seq
        dma("start", jnp.where(wrap, seq + 1, seq), jnp.where(wrap, 0, step + 1), 1 - slot)

    return slot


def _moba_sample_body(pt_ref, q_ref, kn_ref, vn_ref, k_hbm, v_hbm, o_ref, buf, sems, s_ref, bsum_ref, l_ref,
                      acc_ref, *, pp, n_steps, page0):
    seq, j = pl.program_id(0), pl.program_id(1)
    dec_seq = q_ref.shape[0]
    scale = DH_A ** -0.5
    rows = H_A * dec_seq
    pages_per_blk = MOBA_BLOCK // PAGE_SIZE
    blks_per_step = pp // pages_per_blk
    n_blk = bsum_ref.shape[0]

    def dma(action, sq, step, slot):
        @pl.when(step < n_steps)
        def _k():
            _page_dma(action, pt_ref, sq, step * pp, slot, pp, page0, ((k_hbm, PAGE_SIZE, buf, sems),))

        @pl.when(step >= n_steps)
        def _v():
            _page_dma(action, pt_ref, sq, (step - n_steps) * pp, slot, pp, page0, ((v_hbm, PAGE_SIZE, buf, sems),))

    slot = _paged_prefetch(dma, 2 * n_steps)
    dma("wait", seq, j, slot)

    @pl.when(j < n_steps)
    def _keys():
        q = _stack_heads(q_ref, H_A, DH_A).astype(BF16)
        for c in range(blks_per_step):
            bs = jnp.zeros((1, DH_A), F32)
            for u in range(pages_per_blk):
                pg = c * pages_per_blk + u
                k = buf[slot, pg]
                s_ref[j * pp + pg] = _dot_nt(q, k.astype(BF16))
                bs = bs + jnp.sum(k, axis=0, keepdims=True)
            bsum_ref[pl.ds(j * blks_per_step + c, 1), :] = bs

    @pl.when(j == n_steps)
    def _softmax():
        qf = _stack_heads(q_ref, H_A, DH_A)
        gate = _dot_nt(qf, bsum_ref[...] * (1.0 / MOBA_BLOCK), precision=HIGHEST)
        sel = _top_blocks(gate, n_blk)
        k_new = _pad_rows(kn_ref[...], LANES).astype(BF16)
        s_new = _new_token_scores(_dot_nt(qf.astype(BF16), k_new) * scale, dec_seq)
        mx = s_new
        for n in range(n_blk):
            ok = sel[:, n:n + 1] > 0.5
            for u in range(pages_per_blk):
                pg = n * pages_per_blk + u
                sm = jnp.where(ok, s_ref[pg] * scale, NEG)
                s_ref[pg] = sm
                mx = jnp.maximum(mx, sm)
        m = jnp.max(mx, axis=1, keepdims=True)
        p_new = jnp.exp(s_new - m)
        lsum = p_new
        for pg in range(n_blk * pages_per_blk):
            p = jnp.exp(s_ref[pg] - m)
            s_ref[pg] = p
            lsum = lsum + p
        l_ref[...] = jnp.sum(lsum, axis=1, keepdims=True)
        acc_ref[...] = _new_token_values(p_new, vn_ref[...])

    @pl.when(j >= n_steps)
    def _values():
        jv = j - n_steps
        acc = acc_ref[...]
        for pg in range(pp):
            acc = acc + jnp.dot(s_ref[jv * pp + pg].astype(BF16), buf[slot, pg].astype(BF16),
                                preferred_element_type=F32)
        acc_ref[...] = acc

    @pl.when(j == 2 * n_steps - 1)
    def _fin():
        _unstack_heads(acc_ref[...] / l_ref[...], o_ref, H_A, DH_A)


def moba_sample(z, cache_k, cache_v, li, page_table, row0, dec_seq, q_col, k_col, v_col):
    n_seq, n_pages = page_table.shape
    assert (n_pages * PAGE_SIZE) % MOBA_BLOCK == 0 and dec_seq <= MOBA_BLOCK and row0 % dec_seq == 0
    pp = _tile(n_pages, 16, MOBA_BLOCK // PAGE_SIZE)
    n_steps = n_pages // pp
    n_blk = n_pages * PAGE_SIZE // MOBA_BLOCK
    rb0 = row0 // dec_seq
    rows = H_A * dec_seq
    wq = H_A * DH_A
    hbm = pl.BlockSpec(memory_space=pl.ANY)
    grid_spec = pltpu.PrefetchScalarGridSpec(
        num_scalar_prefetch=1,
        grid=(n_seq, 2 * n_steps),
        in_specs=[pl.BlockSpec((dec_seq, wq), lambda s, j, pt: (rb0 + s, q_col // wq)),
                  pl.BlockSpec((dec_seq, DH_A), lambda s, j, pt: (rb0 + s, k_col // DH_A)),
                  pl.BlockSpec((dec_seq, DH_A), lambda s, j, pt: (rb0 + s, v_col // DH_A)), hbm, hbm],
        out_specs=pl.BlockSpec((dec_seq, wq), lambda s, j, pt: (s, 0)),
        scratch_shapes=[pltpu.VMEM((2, pp, PAGE_SIZE, DH_A), F32), pltpu.SemaphoreType.DMA((2,)),
                        pltpu.VMEM((n_pages, rows, PAGE_SIZE), F32),
                        pltpu.VMEM((n_blk, DH_A), F32), pltpu.VMEM((rows, 1), F32),
                        pltpu.VMEM((rows, DH_A), F32)],
    )
    return pl.pallas_call(
        functools.partial(_moba_sample_body, pp=pp, n_steps=n_steps, page0=li * cache_k.shape[1]),
        grid_spec=grid_spec,
        out_shape=jax.ShapeDtypeStruct((n_seq * dec_seq, wq), F32),
        compiler_params=_cparams("arbitrary", "arbitrary"),
        name="moba_sample",
    )(page_table, z, z, z, _page_rows(cache_k, PAGE_SIZE), _page_rows(cache_v, PAGE_SIZE))


def _split3(x):
    hi = x.astype(BF16)
    r1 = x - hi.astype(F32)
    mid = r1.astype(BF16)
    lo = (r1 - mid.astype(F32)).astype(BF16)
    return hi, mid, lo


def _cumsum_rows(x):
    n = x.shape[0]
    x = _pad_rows(x, LANES)
    r = lax.broadcasted_iota(jnp.int32, (n, LANES), 0)
    c = lax.broadcasted_iota(jnp.int32, (n, LANES), 1)
    tril = jnp.where(c <= r, 1.0, 0.0).astype(BF16)
    hi, mid, lo = _split3(x)
    return (jnp.dot(tril, hi, preferred_element_type=F32) + jnp.dot(tril, mid, preferred_element_type=F32)
            + jnp.dot(tril, lo, preferred_element_type=F32))


def _gla_chunk(q, kk, v, b, st):
    c = q.shape[0]
    o_inter = _dot_nt(_pad_rows(q * jnp.exp(b), BF16_ROWS).astype(BF16), st.astype(BF16))[:c]
    n_grp = c // SUBLANES
    row = lax.broadcasted_iota(jnp.int32, (SUBLANES, 1), 0)
    parts = [o_inter[g * SUBLANES:(g + 1) * SUBLANES] for g in range(n_grp)]
    for s in range(c):
        b_s, k_s, v_s = b[s:s + 1], kk[s:s + 1], v[s:s + 1]
        for g in range(s // SUBLANES, n_grp):
            sl = slice(g * SUBLANES, (g + 1) * SUBLANES)
            d = b[sl] - b_s
            if g == s // SUBLANES:
                d = jnp.where(row >= s - g * SUBLANES, d, -jnp.inf)
            att = jnp.sum(q[sl] * jnp.exp(d) * k_s, axis=-1, keepdims=True)
            parts[g] = parts[g] + att * v_s
    o = jnp.concatenate(parts, axis=0) if n_grp > 1 else parts[0]
    b_last = b[c - 1:c]
    kd = kk * jnp.exp(b_last - b)
    st_new = st * jnp.exp(b_last) + _dot_tn(_pad_rows(v, LANES).astype(BF16), _pad_rows(kd, LANES).astype(BF16))
    return o, st_new


def _hgrn2_body(q_ref, f_ref, i_ref, g_ref, lb_ref, gn_ref, *refs, li, heads, chunk, has_state):
    if has_state:
        s0_ref, o_ref, so_ref, st_ref, cs_ref = refs
    else:
        o_ref, so_ref, st_ref, cs_ref = refs
    t = pl.program_id(2)
    tb = q_ref.shape[0]
    sub = min(LANES, tb)
    n_chunks = tb // chunk
    chunks_per_sub = sub // chunk

    @pl.when(t == 0)
    def _init():
        for h in range(heads):
            st_ref[h] = s0_ref[0, h].T if has_state else jnp.zeros((DV_B, DK_B), F32)

    for h in range(heads):
        hs = slice(h * DK_B, (h + 1) * DK_B)
        lbr = lb_ref[:, hs]
        e = jnp.exp(lbr - jnp.max(lbr, axis=0, keepdims=True))
        sm = e / jnp.sum(e, axis=0, keepdims=True)
        lb = jnp.sum(sm[:li + 1], axis=0, keepdims=True)
        for u in range(tb // sub):
            rs = slice(u * sub, (u + 1) * sub)
            cs_ref[rs, :] = _cumsum_rows(jnp.log(lb + (1.0 - lb) * _sigmoid(f_ref[rs, hs])))

        def chunk_step(ci, carry, hs=hs, lb=lb, h=h):
            r0 = pl.multiple_of(ci * chunk, chunk)
            rs = pl.ds(r0, chunk)
            base = cs_ref[pl.ds(jnp.maximum(r0 - 1, 0), 1), :]
            base = jnp.where(ci % chunks_per_sub == 0, 0.0, base)
            q_raw = q_ref[rs, hs]
            q = q_raw * _sigmoid(q_raw)
            fg = lb + (1.0 - lb) * _sigmoid(f_ref[rs, hs])
            o, st_new = _gla_chunk(q, 1.0 - fg, i_ref[rs, hs], cs_ref[rs, :] - base, st_ref[h])
            st_ref[h] = st_new
            g_raw = g_ref[rs, hs]
            o_ref[rs, hs] = _rms(o, gn_ref[...], NORM_EPS) * (g_raw * _sigmoid(g_raw))
            return carry

        lax.fori_loop(0, n_chunks, chunk_step, 0)

    @pl.when(t == pl.num_programs(2) - 1)
    def _fin():
        for h in range(heads):
            so_ref[0, h] = st_ref[h].T


def hgrn2(z, lb_raw, gnorm, li, *, n, t, row0, cols, state):
    chunk = min(32, t)
    tb = _tile(t, 256, chunk)
    heads = H_B if t < 64 else 1
    hw = heads * DK_B
    n_tb = t // tb
    assert row0 % tb == 0 and all(c % hw == 0 for c in cols)
    rb0 = row0 // tb
    n_lb = lb_raw.shape[0]

    def zspec(col):
        return pl.BlockSpec((tb, hw), lambda b, h, c: (rb0 + b * n_tb + c, col // hw + h))

    in_specs = [zspec(c) for c in cols] + [
        pl.BlockSpec((n_lb, hw), lambda b, h, c: (0, h)),
        pl.BlockSpec((1, DV_B), lambda b, h, c: (0, 0))]
    args = [z, z, z, z, lb_raw, gnorm.reshape(1, DV_B)]
    if state is not None:
        in_specs.append(pl.BlockSpec((1, heads, DK_B, DV_B), lambda b, h, c: (b, h, 0, 0)))
        args.append(state)
    return pl.pallas_call(
        functools.partial(_hgrn2_body, li=li, heads=heads, chunk=chunk, has_state=state is not None),
        grid=(n, H_B // heads, n_tb),
        in_specs=in_specs,
        out_specs=[pl.BlockSpec((tb, hw), lambda b, h, c: (b * n_tb + c, h)),
                   pl.BlockSpec((1, heads, DK_B, DV_B), lambda b, h, c: (b, h, 0, 0))],
        out_shape=[jax.ShapeDtypeStruct((n * t, H_B * DV_B), F32),
                   jax.ShapeDtypeStruct((n, H_B, DK_B, DV_B), F32)],
        scratch_shapes=[pltpu.VMEM((heads, DV_B, DK_B), F32), pltpu.VMEM((tb, DK_B), F32)],
        compiler_params=_cparams("parallel", "parallel", "arbitrary"),
        name="hgrn2",
    )(*args)


def _mla_prep_body(cq_ref, ckv_ref, kpe_ref, cosq_ref, sinq_ref, qn_ref, kvn_ref, wn_ref, wp_ref, wr_ref,
                   wuk_ref, qlat_ref, qpe_ref, lat_ref, kr_ref):
    qn = _rms(cq_ref[...], qn_ref[...], NORM_EPS).astype(BF16)
    q_nope = jnp.dot(qn, wn_ref[...], preferred_element_type=F32)
    q_pe = jnp.dot(qn, wp_ref[...], preferred_element_type=F32)
    q_rot = jnp.dot(qn, wr_ref[...], preferred_element_type=F32)
    cosq, sinq = cosq_ref[...], sinq_ref[...]
    qpe_ref[...] = q_pe * cosq + q_rot * sinq
    for h in range(H_C):
        qh = q_nope[:, h * DN_C:(h + 1) * DN_C].astype(BF16)
        qlat_ref[:, h * DC_KV:(h + 1) * DC_KV] = jnp.dot(qh, wuk_ref[h], preferred_element_type=F32)
    lat_ref[...] = _rms(ckv_ref[...], kvn_ref[...], NORM_EPS)
    kp = kpe_ref[...]
    kr_ref[...] = kp[:, :DR_C] * cosq[:, :DR_C] + kp[:, DR_C:] * sinq[:, :DR_C]


def mla_prep(z, cos_t, sin_t, q_norm, kv_norm, w_nope, w_pe, w_rot, w_uk, cq_col, ckv_col, kpe_col):
    m = z.shape[0]
    tm = _tile(m, 256, SUBLANES)
    full = lambda a: pl.BlockSpec(a.shape, lambda i: (0,) * a.ndim)
    qn2, kvn2 = q_norm.reshape(1, DC_Q), kv_norm.reshape(1, DC_KV)
    return pl.pallas_call(
        _mla_prep_body,
        grid=(m // tm,),
        in_specs=[pl.BlockSpec((tm, DC_Q), lambda i: (i, cq_col // DC_Q)),
                  pl.BlockSpec((tm, DC_KV), lambda i: (i, ckv_col // DC_KV)),
                  pl.BlockSpec((tm, 2 * DR_C), lambda i: (i, kpe_col // (2 * DR_C))),
                  pl.BlockSpec((tm, H_C * DR_C), lambda i: (i, 0)),
                  pl.BlockSpec((tm, H_C * DR_C), lambda i: (i, 0)),
                  full(qn2), full(kvn2), full(w_nope), full(w_pe), full(w_rot), full(w_uk)],
        out_specs=[pl.BlockSpec((tm, H_C * DC_KV), lambda i: (i, 0)),
                   pl.BlockSpec((tm, H_C * DR_C), lambda i: (i, 0)),
                   pl.BlockSpec((tm, DC_KV), lambda i: (i, 0)),
                   pl.BlockSpec((tm, DR_C), lambda i: (i, 0))],
        out_shape=[jax.ShapeDtypeStruct((m, H_C * DC_KV), F32),
                   jax.ShapeDtypeStruct((m, H_C * DR_C), F32),
                   jax.ShapeDtypeStruct((m, DC_KV), F32),
                   jax.ShapeDtypeStruct((m, DR_C), F32)],
        compiler_params=_cparams("parallel"),
        name="mla_prep",
    )(z, z, z, cos_t, sin_t, qn2, kvn2, w_nope, w_pe, w_rot, w_uk)


def _q_scratch_dtype(tq):
    return BF16 if tq % BF16_ROWS == 0 else F32


def _mla_stack_q(qlat_ref, qpe_ref, qsl_ref, qsp_ref):
    tq = qlat_ref.shape[0]
    for h in range(H_C):
        qsl_ref[h * tq:(h + 1) * tq, :] = qlat_ref[:, h * DC_KV:(h + 1) * DC_KV].astype(qsl_ref.dtype)
        qsp_ref[h * tq:(h + 1) * tq, :] = qpe_ref[:, h * DR_C:(h + 1) * DR_C].astype(qsp_ref.dtype)


def _mla_finish(acc_ref, l_ref, wuv_ref, o_ref):
    tq = o_ref.shape[0]
    o = acc_ref[...] / l_ref[...]
    for h in range(H_C):
        oh = _pad_rows(o[h * tq:(h + 1) * tq], BF16_ROWS).astype(BF16)
        o_ref[:, h * DV_C:(h + 1) * DV_C] = jnp.dot(oh, wuv_ref[h], preferred_element_type=F32)[:tq]


def _diff_stack_q(dq_ref, qs_ref):
    tq = dq_ref.shape[0]
    qs_ref[...] = jnp.zeros_like(qs_ref)
    for jx in range(2):
        for r in range(H_D):
            r0 = (jx * H_D + r) * tq
            c0 = r * 2 * DH_D + jx * DH_D
            qs_ref[r0:r0 + tq, jx * DH_D:(jx + 1) * DH_D] = dq_ref[:, c0:c0 + DH_D].astype(qs_ref.dtype)


def _diff_finish(acc_ref, l_ref, lq1_ref, lk1_ref, lq2_ref, lk2_ref, sub_ref, o_ref, lam_init):
    tq = o_ref.shape[0]
    half = H_D * tq
    lam = (jnp.exp(jnp.sum(lq1_ref[...] * lk1_ref[...], axis=-1, keepdims=True))
           - jnp.exp(jnp.sum(lq2_ref[...] * lk2_ref[...], axis=-1, keepdims=True)) + lam_init)
    o = acc_ref[...] / l_ref[...]
    o = o[:half] - lam * o[half:]
    o = _rms(o, sub_ref[...], SUBLN_EPS) * (1.0 - lam_init)
    _unstack_heads(o, o_ref, H_D, 2 * DH_D)


def _causal_ok(rows, tq, tk, q0, k0):
    tok = q0 + lax.broadcasted_iota(jnp.int32, (rows, tk), 0) % tq
    key = k0 + lax.broadcasted_iota(jnp.int32, (rows, tk), 1)
    return key <= tok


def _causal_dispatch(step, q0, tq, k0, tk):
    last_key = k0 + tk - 1

    @pl.when(last_key <= q0)
    def _full():
        step(False)

    @pl.when(jnp.logical_and(last_key > q0, k0 <= q0 + tq - 1))
    def _partial():
        step(True)


def _flash_init(m_ref, l_ref, acc_ref):
    m_ref[...] = jnp.full(m_ref.shape, NEG, F32)
    l_ref[...] = jnp.zeros_like(l_ref)
    acc_ref[...] = jnp.zeros_like(acc_ref)


def _mla_prompt_body(qlat_ref, qpe_ref, lat_ref, kr_ref, wuv_ref, o_ref, qsl_ref, qsp_ref, m_ref, l_ref, acc_ref):
    i, j = pl.program_id(1), pl.program_id(2)
    tq, tk = qlat_ref.shape[0], lat_ref.shape[0]
    scale = (DN_C + DR_C) ** -0.5

    @pl.when(j == 0)
    def _init():
        _mla_stack_q(qlat_ref, qpe_ref, qsl_ref, qsp_ref)
        _flash_init(m_ref, l_ref, acc_ref)

    def step(masked):
        lat = lat_ref[...].astype(BF16)
        s = (_dot_nt(qsl_ref[...].astype(BF16), lat)
             + _dot_nt(qsp_ref[...].astype(BF16), kr_ref[...].astype(BF16))) * scale
        if masked:
            s = jnp.where(_causal_ok(H_C * tq, tq, tk, i * tq, j * tk), s, NEG)
        alpha, p = _online_softmax(s, m_ref, l_ref)
        acc_ref[...] = alpha * acc_ref[...] + jnp.dot(p.astype(BF16), lat, preferred_element_type=F32)

    _causal_dispatch(step, i * tq, tq, j * tk, tk)

    @pl.when(j == pl.num_programs(2) - 1)
    def _fin():
        _mla_finish(acc_ref, l_ref, wuv_ref, o_ref)


def mla_prompt(q_lat, q_pe, lat, kr, w_uv, batch, seq):
    tq = _tile(seq, 128, BF16_ROWS)
    tk = _tile(seq, 512, LANES)
    nq, nk = seq // tq, seq // tk
    rows = H_C * tq
    kv_idx = lambda b, i, j: (b * nk + jnp.minimum(j, (i * tq + tq - 1) // tk), 0)
    qdt = _q_scratch_dtype(tq)
    return pl.pallas_call(
        _mla_prompt_body,
        grid=(batch, nq, nk),
        in_specs=[pl.BlockSpec((tq, H_C * DC_KV), lambda b, i, j: (b * nq + i, 0)),
                  pl.BlockSpec((tq, H_C * DR_C), lambda b, i, j: (b * nq + i, 0)),
                  pl.BlockSpec((tk, DC_KV), kv_idx),
                  pl.BlockSpec((tk, DR_C), kv_idx),
                  pl.BlockSpec(w_uv.shape, lambda b, i, j: (0, 0, 0))],
        out_specs=pl.BlockSpec((tq, H_C * DV_C), lambda b, i, j: (b * nq + i, 0)),
        out_shape=jax.ShapeDtypeStruct((batch * seq, H_C * DV_C), F32),
        scratch_shapes=[pltpu.VMEM((rows, DC_KV), qdt), pltpu.VMEM((rows, DR_C), qdt),
                        pltpu.VMEM((rows, 1), F32), pltpu.VMEM((rows, 1), F32),
                        pltpu.VMEM((rows, DC_KV), F32)],
        compiler_params=_cparams("parallel", "parallel", "arbitrary"),
        name="mla_prompt",
    )(q_lat, q_pe, lat, kr, w_uv)


def _diff_prompt_body(dq_ref, dk_ref, dv_ref, lq1_ref, lk1_ref, lq2_ref, lk2_ref, sub_ref, o_ref,
                      qs_ref, m_ref, l_ref, acc_ref, *, lam_init):
    i, j = pl.program_id(1), pl.program_id(2)
    tq, tk = dq_ref.shape[0], dk_ref.shape[0]
    scale = DH_D ** -0.5

    @pl.when(j == 0)
    def _init():
        _diff_stack_q(dq_ref, qs_ref)
        _flash_init(m_ref, l_ref, acc_ref)

    def step(masked):
        s = _dot_nt(qs_ref[...].astype(BF16), dk_ref[...].astype(BF16)) * scale
        if masked:
            s = jnp.where(_causal_ok(2 * H_D * tq, tq, tk, i * tq, j * tk), s, NEG)
        alpha, p = _online_softmax(s, m_ref, l_ref)
        acc_ref[...] = alpha * acc_ref[...] + jnp.dot(p.astype(BF16), dv_ref[...].astype(BF16),
                                                      preferred_element_type=F32)

    _causal_dispatch(step, i * tq, tq, j * tk, tk)

    @pl.when(j == pl.num_programs(2) - 1)
    def _fin():
        _diff_finish(acc_ref, l_ref, lq1_ref, lk1_ref, lq2_ref, lk2_ref, sub_ref, o_ref, lam_init)


def diff_prompt(z, lams, subln, lam_init, batch, seq, dq_col, dk_col, dv_col):
    tq = _tile(seq, 256, BF16_ROWS)
    tk = _tile(seq, 512, LANES)
    nq, nk = seq // tq, seq // tk
    rows = 2 * H_D * tq
    wq, wkv = H_D * 2 * DH_D, 2 * DH_D
    kv_blk = lambda b, i, j: b * nk + jnp.minimum(j, (i * tq + tq - 1) // tk)
    vec = lambda a: pl.BlockSpec(a.shape, lambda b, i, j: (0, 0))
    return pl.pallas_call(
        functools.partial(_diff_prompt_body, lam_init=lam_init),
        grid=(batch, nq, nk),
        in_specs=[pl.BlockSpec((tq, wq), lambda b, i, j: (b * nq + i, dq_col // wq)),
                  pl.BlockSpec((tk, wkv), lambda b, i, j: (kv_blk(b, i, j), dk_col // wkv)),
                  pl.BlockSpec((tk, wkv), lambda b, i, j: (kv_blk(b, i, j), dv_col // wkv))]
                 + [vec(a) for a in lams] + [vec(subln)],
        out_specs=pl.BlockSpec((tq, wq), lambda b, i, j: (b * nq + i, 0)),
        out_shape=jax.ShapeDtypeStruct((batch * seq, wq), F32),
        scratch_shapes=[pltpu.VMEM((rows, wkv), _q_scratch_dtype(tq)), pltpu.VMEM((rows, 1), F32),
                        pltpu.VMEM((rows, 1), F32), pltpu.VMEM((rows, wkv), F32)],
        compiler_params=_cparams("parallel", "parallel", "arbitrary"),
        name="diff_prompt",
    )(z, z, z, *lams, subln)


def _new_token_init(s_new, v_new, m_ref, l_ref, acc_ref, dec_seq):
    s_new = _new_token_scores(s_new, dec_seq)
    m = jnp.max(s_new, axis=1, keepdims=True)
    p = jnp.exp(s_new - m)
    m_ref[...] = m
    l_ref[...] = jnp.sum(p, axis=1, keepdims=True)
    acc_ref[...] = _new_token_values(p, v_new)


def _mla_sample_body(pt_ref, qlat_ref, qpe_ref, nlat_ref, nkr_ref, wuv_ref, lat_hbm, kr_hbm, o_ref,
                     latbuf, krbuf, sems, qsl_ref, qsp_ref, m_ref, l_ref, acc_ref, *, pp, page0):
    j = pl.program_id(1)
    n_steps = pl.num_programs(1)
    dec_seq = qlat_ref.shape[0]
    scale = (DN_C + DR_C) ** -0.5
    streams = ((lat_hbm, PAGE_SIZE, latbuf, sems.at[0]), (kr_hbm, PAGE_SIZE, krbuf, sems.at[1]))
    dma = lambda action, seq, step, slot: _page_dma(action, pt_ref, seq, step * pp, slot, pp, page0, streams)
    slot = _paged_prefetch(dma, n_steps)

    @pl.when(j == 0)
    def _init():
        _mla_stack_q(qlat_ref, qpe_ref, qsl_ref, qsp_ref)
        nlat = nlat_ref[...]
        s_new = (_dot_nt(qsl_ref[...].astype(BF16), _pad_rows(nlat, LANES).astype(BF16))
                 + _dot_nt(qsp_ref[...].astype(BF16), _pad_rows(nkr_ref[...], LANES).astype(BF16))) * scale
        _new_token_init(s_new, nlat, m_ref, l_ref, acc_ref, dec_seq)

    dma("wait", pl.program_id(0), j, slot)
    lat = latbuf[slot].reshape(pp * PAGE_SIZE, DC_KV).astype(BF16)
    kr = krbuf[slot].reshape(pp * PAGE_SIZE, DR_C).astype(BF16)
    s = (_dot_nt(qsl_ref[...].astype(BF16), lat) + _dot_nt(qsp_ref[...].astype(BF16), kr)) * scale
    alpha, p = _online_softmax(s, m_ref, l_ref)
    acc_ref[...] = alpha * acc_ref[...] + jnp.dot(p.astype(BF16), lat, preferred_element_type=F32)

    @pl.when(j == n_steps - 1)
    def _fin():
        _mla_finish(acc_ref, l_ref, wuv_ref, o_ref)


def _page_rows(cache, rows_per_page):
    n = cache.shape[0] * cache.shape[1] * rows_per_page
    return cache.reshape(n, cache.size // n)


def mla_sample(q_lat, q_pe, lat, kr, w_uv, cache_lat, cache_kr, li, page_table, row0, dec_seq):
    n_seq, n_pages = page_table.shape
    pp = _tile(n_pages, 16, 1)
    rb0 = row0 // dec_seq
    rows = H_C * dec_seq
    qrow = lambda s, j, pt: (rb0 + s, 0)
    qdt = _q_scratch_dtype(dec_seq)
    hbm = pl.BlockSpec(memory_space=pl.ANY)
    grid_spec = pltpu.PrefetchScalarGridSpec(
        num_scalar_prefetch=1,
        grid=(n_seq, n_pages // pp),
        in_specs=[pl.BlockSpec((dec_seq, H_C * DC_KV), qrow),
                  pl.BlockSpec((dec_seq, H_C * DR_C), qrow),
                  pl.BlockSpec((dec_seq, DC_KV), qrow),
                  pl.BlockSpec((dec_seq, DR_C), qrow),
                  pl.BlockSpec(w_uv.shape, lambda s, j, pt: (0, 0, 0)), hbm, hbm],
        out_specs=pl.BlockSpec((dec_seq, H_C * DV_C), lambda s, j, pt: (s, 0)),
        scratch_shapes=[pltpu.VMEM((2, pp, PAGE_SIZE, DC_KV), F32), pltpu.VMEM((2, pp, PAGE_SIZE, DR_C), F32),
                        pltpu.SemaphoreType.DMA((2, 2)),
                        pltpu.VMEM((rows, DC_KV), qdt), pltpu.VMEM((rows, DR_C), qdt),
                        pltpu.VMEM((rows, 1), F32), pltpu.VMEM((rows, 1), F32),
                        pltpu.VMEM((rows, DC_KV), F32)],
    )
    return pl.pallas_call(
        functools.partial(_mla_sample_body, pp=pp, page0=li * cache_lat.shape[1]),
        grid_spec=grid_spec,
        out_shape=jax.ShapeDtypeStruct((n_seq * dec_seq, H_C * DV_C), F32),
        compiler_params=_cparams("arbitrary", "arbitrary"),
        name="mla_sample",
    )(page_table, q_lat, q_pe, lat, kr, w_uv, _page_rows(cache_lat, PAGE_SIZE), _page_rows(cache_kr, PAGE_SIZE))


def _deinterleaved(buf, slot, pp):
    halves = [buf[slot, :, pl.ds(c, PAGE_SIZE, stride=2), :].reshape(pp * PAGE_SIZE, DH_D) for c in range(2)]
    return jnp.concatenate(halves, axis=1)


def _diff_sample_body(pt_ref, dq_ref, nk_ref, nv_ref, lq1_ref, lk1_ref, lq2_ref, lk2_ref, sub_ref, k_hbm, v_hbm,
                      o_ref, kbuf, vbuf, sems, qs_ref, m_ref, l_ref, acc_ref, *, pp, page0, lam_init):
    j = pl.program_id(1)
    n_steps = pl.num_programs(1)
    dec_seq = dq_ref.shape[0]
    scale = DH_D ** -0.5
    rows_per_page = 2 * PAGE_SIZE
    streams = ((k_hbm, rows_per_page, kbuf, sems.at[0]), (v_hbm, rows_per_page, vbuf, sems.at[1]))
    dma = lambda action, seq, step, slot: _page_dma(action, pt_ref, seq, step * pp, slot, pp, page0, streams)
    slot = _paged_prefetch(dma, n_steps)

    @pl.when(j == 0)
    def _init():
        _diff_stack_q(dq_ref, qs_ref)
        s_new = _dot_nt(qs_ref[...].astype(BF16), _pad_rows(nk_ref[...], LANES).astype(BF16)) * scale
        _new_token_init(s_new, nv_ref[...], m_ref, l_ref, acc_ref, dec_seq)

    dma("wait", pl.program_id(0), j, slot)
    k = _deinterleaved(kbuf, slot, pp).astype(BF16)
    v = _deinterleaved(vbuf, slot, pp).astype(BF16)
    s = _dot_nt(qs_ref[...].astype(BF16), k) * scale
    alpha, p = _online_softmax(s, m_ref, l_ref)
    acc_ref[...] = alpha * acc_ref[...] + jnp.dot(p.astype(BF16), v, preferred_element_type=F32)

    @pl.when(j == n_steps - 1)
    def _fin():
        _diff_finish(acc_ref, l_ref, lq1_ref, lk1_ref, lq2_ref, lk2_ref, sub_ref, o_ref, lam_init)


def diff_sample(z, lams, subln, lam_init, cache_k, cache_v, li, page_table, row0, dec_seq, dq_col, dk_col, dv_col):
    n_seq, n_pages = page_table.shape
    pp = _tile(n_pages, 16, 1)
    rb0 = row0 // dec_seq
    rows = 2 * H_D * dec_seq
    wq, wkv = H_D * 2 * DH_D, 2 * DH_D
    vec = lambda a: pl.BlockSpec(a.shape, lambda s, j, pt: (0, 0))
    hbm = pl.BlockSpec(memory_space=pl.ANY)
    grid_spec = pltpu.PrefetchScalarGridSpec(
        num_scalar_prefetch=1,
        grid=(n_seq, n_pages // pp),
        in_specs=[pl.BlockSpec((dec_seq, wq), lambda s, j, pt: (rb0 + s, dq_col // wq)),
                  pl.BlockSpec((dec_seq, wkv), lambda s, j, pt: (rb0 + s, dk_col // wkv)),
                  pl.BlockSpec((dec_seq, wkv), lambda s, j, pt: (rb0 + s, dv_col // wkv))]
                 + [vec(a) for a in lams] + [vec(subln), hbm, hbm],
        out_specs=pl.BlockSpec((dec_seq, wq), lambda s, j, pt: (s, 0)),
        scratch_shapes=[pltpu.VMEM((2, pp, 2 * PAGE_SIZE, DH_D), F32), pltpu.VMEM((2, pp, 2 * PAGE_SIZE, DH_D), F32),
                        pltpu.SemaphoreType.DMA((2, 2)),
                        pltpu.VMEM((rows, wkv), _q_scratch_dtype(dec_seq)), pltpu.VMEM((rows, 1), F32),
                        pltpu.VMEM((rows, 1), F32), pltpu.VMEM((rows, wkv), F32)],
    )
    return pl.pallas_call(
        functools.partial(_diff_sample_body, pp=pp, page0=li * cache_k.shape[1], lam_init=lam_init),
        grid_spec=grid_spec,
        out_shape=jax.ShapeDtypeStruct((n_seq * dec_seq, wq), F32),
        compiler_params=_cparams("arbitrary", "arbitrary"),
        name="diff_sample",
    )(page_table, z, z, z, *lams, subln, _page_rows(cache_k, 2 * PAGE_SIZE), _page_rows(cache_v, 2 * PAGE_SIZE))


def _even_layer(x, li, w_in, w_out, lb_raw, gnorm, g_mix, cache_k, cache_v, state, page_table, dims):
    batch, seq, n_seq, dec_seq = dims
    n_p = batch * seq
    wa, wb = H_A * DH_A, H_B * DK_B
    n_a = wa + 2 * DH_A
    w_cat = jnp.concatenate([w_in[:, n_a:], w_in[:, :n_a]], axis=1).astype(BF16)
    b_cols = tuple(u * wb for u in range(4))
    q_off = 4 * wb
    k_off, v_off = q_off + wa, q_off + wa + DH_A
    z = norm_matmul(x, g_mix, w_cat)
    means = moba_means(z, batch, seq, k_off)
    oa_p = moba_prompt(z, means, batch, seq, q_off, k_off, v_off)
    oa_s = moba_sample(z, cache_k, cache_v, li, page_table, n_p, dec_seq, q_off, k_off, v_off)
    ob_p, st_p = hgrn2(z, lb_raw, gnorm, li, n=batch, t=seq, row0=0, cols=b_cols, state=None)
    ob_s, st_s = hgrn2(z, lb_raw, gnorm, li, n=n_seq, t=dec_seq, row0=n_p, cols=b_cols, state=state[li])
    oa = jnp.concatenate([oa_p, oa_s], axis=0)
    ob = jnp.concatenate([ob_p, ob_s], axis=0)
    x = proj_residual(x, oa, ob, w_out.astype(BF16))
    ak, av = z[:, k_off:k_off + DH_A], z[:, v_off:v_off + DH_A]
    return x, (ak, av, st_p, st_s)


def _rope_tables(pos):
    half = DR_C // 2
    inv = ROPE_THETA ** (-jnp.arange(half, dtype=F32) / half)
    ang = pos.astype(F32)[:, None] * inv[None, :]
    cos = jnp.tile(jnp.cos(ang), (1, 2 * H_C))
    sin = jnp.tile(jnp.sin(ang), (1, 2 * H_C))
    return cos, sin


def _rot_cols(w):
    half = DR_C // 2
    return jnp.concatenate([-w[..., half:], w[..., :half]], axis=-1)


def _odd_layer(x, l, li, w_in, w_out, q_norm, kv_norm, w_uq, w_uk, w_uv, lams, subln, g_mix,
               cache_cl, cache_cr, cache_dk, cache_dv, page_table, pos, dims):
    batch, seq, n_seq, dec_seq = dims
    n_p = batch * seq
    lam_init = 0.8 - 0.6 * math.exp(-0.3 * l)
    wq, wkv = H_D * 2 * DH_D, 2 * DH_D
    o_kpe = DC_Q + DC_KV
    o_dq = o_kpe + DR_C
    w_kpe = w_in[:, o_kpe:o_dq]
    w_cat = jnp.concatenate([w_in[:, :o_kpe], w_in[:, o_dq:], w_kpe, _rot_cols(w_kpe)], axis=1).astype(BF16)
    n_cq, n_ckv, n_dq = 0, DC_Q, DC_Q + DC_KV
    n_dk, n_dv, n_kpe = n_dq + wq, n_dq + wq + wkv, n_dq + wq + 2 * wkv
    z = norm_matmul(x, g_mix, w_cat)

    w3 = w_uq.reshape(DC_Q, H_C, DN_C + DR_C)
    w_nope = w3[:, :, :DN_C].reshape(DC_Q, H_C * DN_C).astype(BF16)
    w_pe3 = w3[:, :, DN_C:]
    w_pe = w_pe3.reshape(DC_Q, H_C * DR_C).astype(BF16)
    w_rot = _rot_cols(w_pe3).reshape(DC_Q, H_C * DR_C).astype(BF16)
    w_uk_t = jnp.transpose(w_uk, (1, 2, 0)).astype(BF16)
    w_uv_t = jnp.transpose(w_uv, (1, 0, 2)).astype(BF16)
    cos_t, sin_t = _rope_tables(pos)
    q_lat, q_pe, lat, kr = mla_prep(z, cos_t, sin_t, q_norm, kv_norm, w_nope, w_pe, w_rot, w_uk_t,
                                    n_cq, n_ckv, n_kpe)
    oc_p = mla_prompt(q_lat, q_pe, lat, kr, w_uv_t, batch, seq)
    oc_s = mla_sample(q_lat, q_pe, lat, kr, w_uv_t, cache_cl, cache_cr, li, page_table, n_p, dec_seq)
    lams2 = tuple(a.reshape(1, DH_D) for a in lams)
    sub2 = subln.reshape(1, 2 * DH_D)
    od_p = diff_prompt(z, lams2, sub2, lam_init, batch, seq, n_dq, n_dk, n_dv)
    od_s = diff_sample(z, lams2, sub2, lam_init, cache_dk, cache_dv, li, page_table, n_p, dec_seq,
                       n_dq, n_dk, n_dv)
    oc = jnp.concatenate([oc_p, oc_s], axis=0)
    od = jnp.concatenate([od_p, od_s], axis=0)
    x = proj_residual(x, oc, od, w_out.astype(BF16))
    return x, (lat, kr, z[:, n_dk:n_dk + wkv], z[:, n_dv:n_dv + wkv])


def kernel(x_prompt, x_sample, cache_a_k, cache_a_v, state_b, cache_c_latent, cache_c_krope, cache_d_k, cache_d_v,
           page_table, norm_mix, norm_ffn, norm_final, w_in_even, w_out_even, hgrn_lb, hgrn_gnorm, w_in_odd,
           w_out_odd, mla_q_norm, mla_kv_norm, w_mla_uq, w_mla_uk, w_mla_uv, diff_lq1, diff_lk1, diff_lq2,
           diff_lk2, diff_subln, w_ffn_gate, w_ffn_up, w_ffn_down):
    batch, seq, d = x_prompt.shape
    n_seq, dec_seq, _ = x_sample.shape
    depth = norm_mix.shape[0]
    n_p = batch * seq
    past_len = page_table.shape[1] * PAGE_SIZE
    dims = (batch, seq, n_seq, dec_seq)
    pos = jnp.concatenate([jnp.tile(jnp.arange(seq, dtype=jnp.int32), batch),
                           jnp.tile(past_len + jnp.arange(dec_seq, dtype=jnp.int32), n_seq)])
    x = jnp.concatenate([x_prompt.reshape(n_p, d), x_sample.reshape(n_seq * dec_seq, d)], axis=0)
    even, odd = [], []
    for l in range(depth):
        li = l // 2
        if l % 2 == 0:
            x, st = _even_layer(x, li, w_in_even[li], w_out_even[li], hgrn_lb, hgrn_gnorm[li], norm_mix[l],
                                cache_a_k, cache_a_v, state_b, page_table, dims)
            even.append(st)
        else:
            lams = (diff_lq1[li], diff_lk1[li], diff_lq2[li], diff_lk2[li])
            x, st = _odd_layer(x, l, li, w_in_odd[li], w_out_odd[li], mla_q_norm[li], mla_kv_norm[li],
                               w_mla_uq[li], w_mla_uk[li], w_mla_uv[li], lams, diff_subln[li], norm_mix[l],
                               cache_c_latent, cache_c_krope, cache_d_k, cache_d_v, page_table, pos, dims)
            odd.append(st)
        x = ffn(x, norm_ffn[l], norm_final, w_ffn_gate[l].astype(BF16), w_ffn_up[l].astype(BF16),
                w_ffn_down[l].astype(BF16), final_norm=(l == depth - 1))

    def split(a, tail):
        return a[:n_p].reshape((batch, seq) + tail), a[n_p:].reshape((n_seq, dec_seq) + tail)

    stack = lambda xs: jnp.stack(xs, axis=0)
    y_p, y_s = split(x, (d,))
    ak = [split(e[0], (1, DH_A)) for e in even]
    av = [split(e[1], (1, DH_A)) for e in even]
    cl = [split(o[0], (DC_KV,)) for o in odd]
    cr = [split(o[1], (DR_C,)) for o in odd]
    dk = [split(o[2], (1, 2 * DH_D)) for o in odd]
    dv = [split(o[3], (1, 2 * DH_D)) for o in odd]
    pick = lambda pairs, u: stack([p[u] for p in pairs])
    return (y_p, y_s,
            pick(ak, 0), pick(av, 0), stack([e[2] for e in even]),
            pick(cl, 0), pick(cr, 0), pick(dk, 0), pick(dv, 0),
            pick(ak, 1), pick(av, 1), stack([e[3] for e in even]),
            pick(cl, 1), pick(cr, 1), pick(dk, 1), pick(dv, 1))
```

```python
import functools
import math

import jax
import jax.numpy as jnp
from jax import lax
from jax.experimental import pallas as pl
from jax.experimental.pallas import tpu as pltpu

F32 = jnp.float32
BF16 = jnp.bfloat16
HIGHEST = lax.Precision.HIGHEST

PAGE_SIZE = 128
H_A, DH_A = 8, 128
MOBA_BLOCK, MOBA_TOPK = 256, 3
H_B, DK_B, DV_B = 8, 128, 128
H_C, DC_Q, DC_KV, DN_C, DR_C, DV_C = 8, 512, 512, 128, 64, 128
ROPE_THETA = 10000.0
H_D, DH_D = 4, 128
NORM_EPS = 1e-6
SUBLN_EPS = 1e-5
NEG = -1e30
BELOW_NEG = -3e38

VMEM_LIMIT_BYTES = 56 * 1024 * 1024
SUBLANES = 8
LANES = 128
BF16_ROWS = 16


def _cparams(*sem):
    return pltpu.CompilerParams(dimension_semantics=sem, vmem_limit_bytes=VMEM_LIMIT_BYTES)


def _tile(n, pref, mult):
    if n <= pref:
        return n
    t = (pref // mult) * mult
    while t >= mult:
        if n % t == 0:
            return t
        t -= mult
    raise ValueError(f"no tile for {n} (pref {pref}, mult {mult})")


def _dot_nt(a, b, precision=None):
    return lax.dot_general(a, b, (((1,), (1,)), ((), ())), preferred_element_type=F32, precision=precision)


def _dot_tn(a, b, precision=None):
    return lax.dot_general(a, b, (((0,), (0,)), ((), ())), preferred_element_type=F32, precision=precision)


def _sigmoid(x):
    return 1.0 / (1.0 + jnp.exp(-x))


def _rms(x, g, eps):
    return x * lax.rsqrt(jnp.mean(x * x, axis=-1, keepdims=True) + eps) * g


def _pad_rows(x, n):
    if x.shape[0] >= n:
        return x
    return jnp.concatenate([x, jnp.zeros((n - x.shape[0], x.shape[1]), x.dtype)], axis=0)


def _online_softmax(s, m_ref, l_ref):
    m_prev = m_ref[...]
    m_new = jnp.maximum(m_prev, jnp.max(s, axis=1, keepdims=True))
    alpha = jnp.exp(m_prev - m_new)
    p = jnp.exp(s - m_new)
    l_ref[...] = alpha * l_ref[...] + jnp.sum(p, axis=1, keepdims=True)
    m_ref[...] = m_new
    return alpha, p


def _norm_matmul_body(x_ref, g_ref, w_ref, o_ref, h_ref):
    @pl.when(pl.program_id(1) == 0)
    def _():
        h_ref[...] = _rms(x_ref[...], g_ref[...], NORM_EPS).astype(BF16)

    o_ref[...] = jnp.dot(h_ref[...], w_ref[...], preferred_element_type=F32)


def norm_matmul(x, g, w):
    m, d = x.shape
    n = w.shape[1]
    tm = _tile(m, 512, SUBLANES)
    tn = _tile(n, 896, LANES)
    return pl.pallas_call(
        _norm_matmul_body,
        grid=(m // tm, n // tn),
        in_specs=[pl.BlockSpec((tm, d), lambda i, j: (i, 0)),
                  pl.BlockSpec((1, d), lambda i, j: (0, 0)),
                  pl.BlockSpec((d, tn), lambda i, j: (0, j))],
        out_specs=pl.BlockSpec((tm, tn), lambda i, j: (i, j)),
        out_shape=jax.ShapeDtypeStruct((m, n), F32),
        scratch_shapes=[pltpu.VMEM((tm, d), BF16)],
        compiler_params=_cparams("parallel", "arbitrary"),
        name="norm_matmul",
    )(x, g.reshape(1, d), w)


def _proj_residual_body(x_ref, a1_ref, a2_ref, w1_ref, w2_ref, o_ref):
    y = jnp.dot(a1_ref[...].astype(BF16), w1_ref[...], preferred_element_type=F32)
    y = y + jnp.dot(a2_ref[...].astype(BF16), w2_ref[...], preferred_element_type=F32)
    o_ref[...] = x_ref[...] + y


def proj_residual(x, a1, a2, w):
    m, d = x.shape
    k1, k2 = a1.shape[1], a2.shape[1]
    assert w.shape == (k1 + k2, d) and k1 == k2
    tm = _tile(m, 512, SUBLANES)
    tn = _tile(d, 1024, LANES)
    return pl.pallas_call(
        _proj_residual_body,
        grid=(m // tm, d // tn),
        in_specs=[pl.BlockSpec((tm, tn), lambda i, j: (i, j)),
                  pl.BlockSpec((tm, k1), lambda i, j: (i, 0)),
                  pl.BlockSpec((tm, k2), lambda i, j: (i, 0)),
                  pl.BlockSpec((k1, tn), lambda i, j: (0, j)),
                  pl.BlockSpec((k2, tn), lambda i, j: (1, j))],
        out_specs=pl.BlockSpec((tm, tn), lambda i, j: (i, j)),
        out_shape=jax.ShapeDtypeStruct((m, d), F32),
        compiler_params=_cparams("parallel", "arbitrary"),
        name="proj_residual",
    )(x, a1, a2, w, w)


def _ffn_body(x_ref, g_ref, gf_ref, wg_ref, wu_ref, wd_ref, o_ref, h_ref, acc_ref, *, final_norm):
    j = pl.program_id(1)

    @pl.when(j == 0)
    def _():
        h_ref[...] = _rms(x_ref[...], g_ref[...], NORM_EPS).astype(BF16)
        acc_ref[...] = jnp.zeros_like(acc_ref)

    h = h_ref[...]
    gate = jnp.dot(h, wg_ref[...], preferred_element_type=F32)
    up = jnp.dot(h, wu_ref[...], preferred_element_type=F32)
    act = (gate * _sigmoid(gate) * up).astype(BF16)
    acc_ref[...] += jnp.dot(act, wd_ref[...], preferred_element_type=F32)

    @pl.when(j == pl.num_programs(1) - 1)
    def _():
        y = x_ref[...] + acc_ref[...]
        if final_norm:
            y = _rms(y, gf_ref[...], NORM_EPS)
        o_ref[...] = y


def ffn(x, g, gf, wg, wu, wd, *, final_norm):
    m, d = x.shape
    f = wg.shape[1]
    tm = _tile(m, 512, SUBLANES)
    tf = _tile(f, 512, LANES)
    return pl.pallas_call(
        functools.partial(_ffn_body, final_norm=final_norm),
        grid=(m // tm, f // tf),
        in_specs=[pl.BlockSpec((tm, d), lambda i, j: (i, 0)),
                  pl.BlockSpec((1, d), lambda i, j: (0, 0)),
                  pl.BlockSpec((1, d), lambda i, j: (0, 0)),
                  pl.BlockSpec((d, tf), lambda i, j: (0, j)),
                  pl.BlockSpec((d, tf), lambda i, j: (0, j)),
                  pl.BlockSpec((tf, d), lambda i, j: (j, 0))],
        out_specs=pl.BlockSpec((tm, d), lambda i, j: (i, 0)),
        out_shape=jax.ShapeDtypeStruct((m, d), F32),
        scratch_shapes=[pltpu.VMEM((tm, d), BF16), pltpu.VMEM((tm, d), F32)],
        compiler_params=_cparams("parallel", "arbitrary"),
        name="ffn",
    )(x, g.reshape(1, d), gf.reshape(1, d), wg, wu, wd)


def _moba_means_body(k_ref, o_ref):
    t, dh = k_ref.shape
    k = k_ref[...].reshape(t // MOBA_BLOCK, MOBA_BLOCK, dh)
    o_ref[0] = jnp.sum(k, axis=1) * (1.0 / MOBA_BLOCK)


def moba_means(z, batch, seq, k_col):
    nb = seq // MOBA_BLOCK
    return pl.pallas_call(
        _moba_means_body,
        grid=(batch,),
        in_specs=[pl.BlockSpec((seq, DH_A), lambda b: (b, k_col // DH_A))],
        out_specs=pl.BlockSpec((1, nb, DH_A), lambda b: (b, 0, 0)),
        out_shape=jax.ShapeDtypeStruct((batch, nb, DH_A), F32),
        compiler_params=_cparams("parallel"),
        name="moba_means",
    )(z)


def _top_blocks(gate, n_valid):
    nb = gate.shape[1]
    col = lax.broadcasted_iota(jnp.int32, gate.shape, 1)
    colf = col.astype(F32)
    valid = col < n_valid
    g = jnp.where(valid, gate, NEG)
    sel = jnp.zeros(gate.shape, F32)
    for _ in range(MOBA_TOPK):
        mx = jnp.max(g, axis=1, keepdims=True)
        idx = jnp.min(jnp.where(g == mx, colf, float(nb)), axis=1, keepdims=True)
        pick = colf == idx
        sel = jnp.where(jnp.logical_and(pick, valid), 1.0, sel)
        g = jnp.where(pick, BELOW_NEG, g)
    return sel


def _stack_heads(q_ref, n_heads, width):
    return jnp.concatenate([q_ref[:, h * width:(h + 1) * width] for h in range(n_heads)], axis=0)


def _unstack_heads(o, o_ref, n_heads, width):
    tq = o_ref.shape[0]
    for h in range(n_heads):
        o_ref[:, h * width:(h + 1) * width] = o[h * tq:(h + 1) * tq]


def _moba_prompt_body(q_ref, k_ref, v_ref, mean_ref, o_ref, qs_ref, sel_ref, m_ref, l_ref, acc_ref):
    i = pl.program_id(1)
    jj = pl.program_id(2)
    tq = MOBA_BLOCK
    scale = DH_A ** -0.5

    @pl.when(jj == 0)
    def _init():
        qf = _stack_heads(q_ref, H_A, DH_A)
        qs_ref[...] = qf.astype(BF16)
        gate = _dot_nt(qf, mean_ref[0], precision=HIGHEST)
        sel_ref[...] = _top_blocks(gate, i)
        m_ref[...] = jnp.full(m_ref.shape, NEG, F32)
        l_ref[...] = jnp.zeros_like(l_ref)
        acc_ref[...] = jnp.zeros_like(acc_ref)

    def step(ok):
        s = _dot_nt(qs_ref[...], k_ref[...].astype(BF16)) * scale
        s = jnp.where(ok, s, NEG)
        alpha, p = _online_softmax(s, m_ref, l_ref)
        acc_ref[...] = alpha * acc_ref[...] + jnp.dot(p.astype(BF16), v_ref[...].astype(BF16),
                                                      preferred_element_type=F32)

    @pl.when(jj == 0)
    def _own():
        shape = (H_A * tq, tq)
        tok = lax.broadcasted_iota(jnp.int32, shape, 0) % tq
        key = lax.broadcasted_iota(jnp.int32, shape, 1)
        step(key <= tok)

    @pl.when(jnp.logical_and(jj > 0, jj <= i))
    def _past():
        n = i - jj
        blk = lax.broadcasted_iota(jnp.int32, sel_ref.shape, 1)
        picked = jnp.sum(jnp.where(blk == n, sel_ref[...], 0.0), axis=1, keepdims=True)
        step(picked > 0.5)

    @pl.when(jj == pl.num_programs(2) - 1)
    def _fin():
        _unstack_heads(acc_ref[...] / l_ref[...], o_ref, H_A, DH_A)


def moba_prompt(z, means, batch, seq, q_col, k_col, v_col):
    nb = seq // MOBA_BLOCK
    rows = H_A * MOBA_BLOCK
    wq = H_A * DH_A
    kv_row = lambda b, i, jj: b * nb + jnp.maximum(i - jj, 0)
    return pl.pallas_call(
        _moba_prompt_body,
        grid=(batch, nb, nb),
        in_specs=[pl.BlockSpec((MOBA_BLOCK, wq), lambda b, i, jj: (b * nb + i, q_col // wq)),
                  pl.BlockSpec((MOBA_BLOCK, DH_A), lambda b, i, jj: (kv_row(b, i, jj), k_col // DH_A)),
                  pl.BlockSpec((MOBA_BLOCK, DH_A), lambda b, i, jj: (kv_row(b, i, jj), v_col // DH_A)),
                  pl.BlockSpec((1, nb, DH_A), lambda b, i, jj: (b, 0, 0))],
        out_specs=pl.BlockSpec((MOBA_BLOCK, wq), lambda b, i, jj: (b * nb + i, 0)),
        out_shape=jax.ShapeDtypeStruct((batch * seq, wq), F32),
        scratch_shapes=[pltpu.VMEM((rows, DH_A), BF16), pltpu.VMEM((rows, nb), F32),
                        pltpu.VMEM((rows, 1), F32), pltpu.VMEM((rows, 1), F32),
                        pltpu.VMEM((rows, DH_A), F32)],
        compiler_params=_cparams("parallel", "parallel", "arbitrary"),
        name="moba_prompt",
    )(z, z, z, means)


def _new_token_scores(s_new, dec_seq):
    rows = s_new.shape[0]
    tok = lax.broadcasted_iota(jnp.int32, (rows, LANES), 0) % dec_seq
    key = lax.broadcasted_iota(jnp.int32, (rows, LANES), 1)
    return jnp.where(key <= tok, s_new, NEG)


def _new_token_values(p_new, v_new):
    acc = p_new[:, 0:1] * v_new[0:1]
    for c in range(1, v_new.shape[0]):
        acc = acc + p_new[:, c:c + 1] * v_new[c:c + 1]
    return acc


def _page_dma(action, pt_ref, seq, first_page, slot, pp, page0, streams):
    if action == "wait":
        for _, _, buf, sem in streams:
            pltpu.make_async_copy(buf.at[slot], buf.at[slot], sem.at[slot]).wait()
        return
    for u in range(pp):
        page = page0 + pt_ref[seq, first_page + u]
        for hbm, rows, buf, sem in streams:
            r0 = pl.multiple_of(page * rows, rows)
            pltpu.make_async_copy(hbm.at[pl.ds(r0, rows)], buf.at[slot, u], sem.at[slot]).start()


def _paged_prefetch(dma, steps_per_seq):
    seq, step = pl.program_id(0), pl.program_id(1)
    g = seq * steps_per_seq + step
    slot = g % 2

    @pl.when(g == 0)
    def _first():
        dma("start", seq, step, slot)

    @pl.when(g + 1 < pl.num_programs(0) * steps_per_seq)
    def _next():
        wrap = step + 1 == steps_per_seq
        dma("start", jnp.where(wrap, seq + 1, seq), jnp.where(wrap, 0, step + 1), 1 - slot)

    return slot


def _moba_sample_body(pt_ref, q_ref, kn_ref, vn_ref, k_hbm, v_hbm, o_ref, buf, sems, s_ref, bsum_ref, l_ref,
                      acc_ref, *, pp, n_steps, page0):
    seq, j = pl.program_id(0), pl.program_id(1)
    dec_seq = q_ref.shape[0]
    scale = DH_A ** -0.5
    rows = H_A * dec_seq
    pages_per_blk = MOBA_BLOCK // PAGE_SIZE
    blks_per_step = pp // pages_per_blk
    n_blk = bsum_ref.shape[0]

    def dma(action, sq, step, slot):
        @pl.when(step < n_steps)
        def _k():
            _page_dma(action, pt_ref, sq, step * pp, slot, pp, page0, ((k_hbm, PAGE_SIZE, buf, sems),))

        @pl.when(step >= n_steps)
        def _v():
            _page_dma(action, pt_ref, sq, (step - n_steps) * pp, slot, pp, page0, ((v_hbm, PAGE_SIZE, buf, sems),))

    slot = _paged_prefetch(dma, 2 * n_steps)
    dma("wait", seq, j, slot)

    @pl.when(j < n_steps)
    def _keys():
        q = _stack_heads(q_ref, H_A, DH_A).astype(BF16)
        for c in range(blks_per_step):
            bs = jnp.zeros((1, DH_A), F32)
            for u in range(pages_per_blk):
                pg = c * pages_per_blk + u
                k = buf[slot, pg]
                s_ref[j * pp + pg] = _dot_nt(q, k.astype(BF16))
                bs = bs + jnp.sum(k, axis=0, keepdims=True)
            bsum_ref[pl.ds(j * blks_per_step + c, 1), :] = bs

    @pl.when(j == n_steps)
    def _softmax():
        qf = _stack_heads(q_ref, H_A, DH_A)
        gate = _dot_nt(qf, bsum_ref[...] * (1.0 / MOBA_BLOCK), precision=HIGHEST)
        sel = _top_blocks(gate, n_blk)
        k_new = _pad_rows(kn_ref[...], LANES).astype(BF16)
        s_new = _new_token_scores(_dot_nt(qf.astype(BF16), k_new) * scale, dec_seq)
        mx = s_new
        for n in range(n_blk):
            ok = sel[:, n:n + 1] > 0.5
            for u in range(pages_per_blk):
                pg = n * pages_per_blk + u
                sm = jnp.where(ok, s_ref[pg] * scale, NEG)
                s_ref[pg] = sm
                mx = jnp.maximum(mx, sm)
        m = jnp.max(mx, axis=1, keepdims=True)
        p_new = jnp.exp(s_new - m)
        lsum = p_new
        for pg in range(n_blk * pages_per_blk):
            p = jnp.exp(s_ref[pg] - m)
            s_ref[pg] = p
            lsum = lsum + p
        l_ref[...] = jnp.sum(lsum, axis=1, keepdims=True)
        acc_ref[...] = _new_token_values(p_new, vn_ref[...])

    @pl.when(j >= n_steps)
    def _values():
        jv = j - n_steps
        acc = acc_ref[...]
        for pg in range(pp):
            acc = acc + jnp.dot(s_ref[jv * pp + pg].astype(BF16), buf[slot, pg].astype(BF16),
                                preferred_element_type=F32)
        acc_ref[...] = acc

    @pl.when(j == 2 * n_steps - 1)
    def _fin():
        _unstack_heads(acc_ref[...] / l_ref[...], o_ref, H_A, DH_A)


def moba_sample(z, cache_k, cache_v, li, page_table, row0, dec_seq, q_col, k_col, v_col):
    n_seq, n_pages = page_table.shape
    assert (n_pages * PAGE_SIZE) % MOBA_BLOCK == 0 and dec_seq <= MOBA_BLOCK and row0 % dec_seq == 0
    pp = _tile(n_pages, 32, MOBA_BLOCK // PAGE_SIZE)
    n_steps = n_pages // pp
    n_blk = n_pages * PAGE_SIZE // MOBA_BLOCK
    rb0 = row0 // dec_seq
    rows = H_A * dec_seq
    wq = H_A * DH_A
    hbm = pl.BlockSpec(memory_space=pl.ANY)
    grid_spec = pltpu.PrefetchScalarGridSpec(
        num_scalar_prefetch=1,
        grid=(n_seq, 2 * n_steps),
        in_specs=[pl.BlockSpec((dec_seq, wq), lambda s, j, pt: (rb0 + s, q_col // wq)),
                  pl.BlockSpec((dec_seq, DH_A), lambda s, j, pt: (rb0 + s, k_col // DH_A)),
                  pl.BlockSpec((dec_seq, DH_A), lambda s, j, pt: (rb0 + s, v_col // DH_A)), hbm, hbm],
        out_specs=pl.BlockSpec((dec_seq, wq), lambda s, j, pt: (s, 0)),
        scratch_shapes=[pltpu.VMEM((2, pp, PAGE_SIZE, DH_A), F32), pltpu.SemaphoreType.DMA((2,)),
                        pltpu.VMEM((n_pages, rows, PAGE_SIZE), F32),
                        pltpu.VMEM((n_blk, DH_A), F32), pltpu.VMEM((rows, 1), F32),
                        pltpu.VMEM((rows, DH_A), F32)],
    )
    return pl.pallas_call(
        functools.partial(_moba_sample_body, pp=pp, n_steps=n_steps, page0=li * cache_k.shape[1]),
        grid_spec=grid_spec,
        out_shape=jax.ShapeDtypeStruct((n_seq * dec_seq, wq), F32),
        compiler_params=_cparams("arbitrary", "arbitrary"),
        name="moba_sample",
    )(page_table, z, z, z, _page_rows(cache_k, PAGE_SIZE), _page_rows(cache_v, PAGE_SIZE))


def _split3(x):
    hi = x.astype(BF16)
    r1 = x - hi.astype(F32)
    mid = r1.astype(BF16)
    lo = (r1 - mid.astype(F32)).astype(BF16)
    return hi, mid, lo


def _cumsum_rows(x):
    n = x.shape[0]
    x = _pad_rows(x, LANES)
    r = lax.broadcasted_iota(jnp.int32, (n, LANES), 0)
    c = lax.broadcasted_iota(jnp.int32, (n, LANES), 1)
    tril = jnp.where(c <= r, 1.0, 0.0).astype(BF16)
    hi, mid, lo = _split3(x)
    return (jnp.dot(tril, hi, preferred_element_type=F32) + jnp.dot(tril, mid, preferred_element_type=F32)
            + jnp.dot(tril, lo, preferred_element_type=F32))


def _gla_chunk(q, kk, v, b, st):
    c = q.shape[0]
    o_inter = _dot_nt(_pad_rows(q * jnp.exp(b), BF16_ROWS).astype(BF16), st.astype(BF16))[:c]
    n_grp = c // SUBLANES
    row = lax.broadcasted_iota(jnp.int32, (SUBLANES, 1), 0)
    parts = [o_inter[g * SUBLANES:(g + 1) * SUBLANES] for g in range(n_grp)]
    for s in range(c):
        b_s, k_s, v_s = b[s:s + 1], kk[s:s + 1], v[s:s + 1]
        for g in range(s // SUBLANES, n_grp):
            sl = slice(g * SUBLANES, (g + 1) * SUBLANES)
            d = b[sl] - b_s
            if g == s // SUBLANES:
                d = jnp.where(row >= s - g * SUBLANES, d, -jnp.inf)
            att = jnp.sum(q[sl] * jnp.exp(d) * k_s, axis=-1, keepdims=True)
            parts[g] = parts[g] + att * v_s
    o = jnp.concatenate(parts, axis=0) if n_grp > 1 else parts[0]
    b_last = b[c - 1:c]
    kd = kk * jnp.exp(b_last - b)
    st_new = st * jnp.exp(b_last) + _dot_tn(_pad_rows(v, LANES).astype(BF16), _pad_rows(kd, LANES).astype(BF16))
    return o, st_new


def _hgrn2_body(q_ref, f_ref, i_ref, g_ref, lb_ref, gn_ref, *refs, li, heads, chunk, has_state):
    if has_state:
        s0_ref, o_ref, so_ref, st_ref, cs_ref = refs
    else:
        o_ref, so_ref, st_ref, cs_ref = refs
    t = pl.program_id(2)
    tb = q_ref.shape[0]
    sub = min(LANES, tb)
    n_chunks = tb // chunk
    chunks_per_sub = sub // chunk

    @pl.when(t == 0)
    def _init():
        for h in range(heads):
            st_ref[h] = s0_ref[0, h].T if has_state else jnp.zeros((DV_B, DK_B), F32)

    lbr = lb_ref[...]
    e = jnp.exp(lbr - jnp.max(lbr, axis=0, keepdims=True))
    sm = e / jnp.sum(e, axis=0, keepdims=True)
    lb_all = jnp.sum(sm[:li + 1], axis=0, keepdims=True)
    for u in range(tb // sub):
        rs = slice(u * sub, (u + 1) * sub)
        cs = _cumsum_rows(jnp.log(lb_all + (1.0 - lb_all) * _sigmoid(f_ref[rs, :])))
        for h in range(heads):
            cs_ref[h, rs, :] = cs[:, h * DK_B:(h + 1) * DK_B]

    def chunk_step(ci, carry):
        r0 = pl.multiple_of(ci * chunk, chunk)
        rs = pl.ds(r0, chunk)
        first = ci % chunks_per_sub == 0
        for h in range(heads):
            hs = slice(h * DK_B, (h + 1) * DK_B)
            lb = lb_all[:, hs]
            base = jnp.where(first, 0.0, cs_ref[h, pl.ds(jnp.maximum(r0 - 1, 0), 1), :])
            q_raw = q_ref[rs, hs]
            q = q_raw * _sigmoid(q_raw)
            fg = lb + (1.0 - lb) * _sigmoid(f_ref[rs, hs])
            o, st_new = _gla_chunk(q, 1.0 - fg, i_ref[rs, hs], cs_ref[h, rs, :] - base, st_ref[h])
            st_ref[h] = st_new
            g_raw = g_ref[rs, hs]
            o_ref[rs, hs] = _rms(o, gn_ref[...], NORM_EPS) * (g_raw * _sigmoid(g_raw))
        return carry

    lax.fori_loop(0, n_chunks, chunk_step, 0)

    @pl.when(t == pl.num_programs(2) - 1)
    def _fin():
        for h in range(heads):
            so_ref[0, h] = st_ref[h].T


def hgrn2(z, lb_raw, gnorm, li, *, n, t, row0, cols, state):
    chunk = min(32, t)
    tb = _tile(t, 256, chunk)
    heads = H_B if t < 64 else 2
    hw = heads * DK_B
    n_tb = t // tb
    assert row0 % tb == 0 and all(c % hw == 0 for c in cols)
    rb0 = row0 // tb
    n_lb = lb_raw.shape[0]

    def zspec(col):
        return pl.BlockSpec((tb, hw), lambda b, h, c: (rb0 + b * n_tb + c, col // hw + h))

    in_specs = [zspec(c) for c in cols] + [
        pl.BlockSpec((n_lb, hw), lambda b, h, c: (0, h)),
        pl.BlockSpec((1, DV_B), lambda b, h, c: (0, 0))]
    args = [z, z, z, z, lb_raw, gnorm.reshape(1, DV_B)]
    if state is not None:
        in_specs.append(pl.BlockSpec((1, heads, DK_B, DV_B), lambda b, h, c: (b, h, 0, 0)))
        args.append(state)
    return pl.pallas_call(
        functools.partial(_hgrn2_body, li=li, heads=heads, chunk=chunk, has_state=state is not None),
        grid=(n, H_B // heads, n_tb),
        in_specs=in_specs,
        out_specs=[pl.BlockSpec((tb, hw), lambda b, h, c: (b * n_tb + c, h)),
                   pl.BlockSpec((1, heads, DK_B, DV_B), lambda b, h, c: (b, h, 0, 0))],
        out_shape=[jax.ShapeDtypeStruct((n * t, H_B * DV_B), F32),
                   jax.ShapeDtypeStruct((n, H_B, DK_B, DV_B), F32)],
        scratch_shapes=[pltpu.VMEM((heads, DV_B, DK_B), F32), pltpu.VMEM((heads, tb, DK_B), F32)],
        compiler_params=_cparams("parallel", "parallel", "arbitrary"),
        name="hgrn2",
    )(*args)


def _mla_prep_body(cq_ref, ckv_ref, kpe_ref, cosq_ref, sinq_ref, qn_ref, kvn_ref, wn_ref, wp_ref, wr_ref,
                   wuk_ref, qlat_ref, qpe_ref, lat_ref, kr_ref):
    qn = _rms(cq_ref[...], qn_ref[...], NORM_EPS).astype(BF16)
    q_nope = jnp.dot(qn, wn_ref[...], preferred_element_type=F32)
    q_pe = jnp.dot(qn, wp_ref[...], preferred_element_type=F32)
    q_rot = jnp.dot(qn, wr_ref[...], preferred_element_type=F32)
    cosq, sinq = cosq_ref[...], sinq_ref[...]
    qpe_ref[...] = q_pe * cosq + q_rot * sinq
    for h in range(H_C):
        qh = q_nope[:, h * DN_C:(h + 1) * DN_C].astype(BF16)
        qlat_ref[:, h * DC_KV:(h + 1) * DC_KV] = jnp.dot(qh, wuk_ref[h], preferred_element_type=F32)
    lat_ref[...] = _rms(ckv_ref[...], kvn_ref[...], NORM_EPS)
    kp = kpe_ref[...]
    kr_ref[...] = kp[:, :DR_C] * cosq[:, :DR_C] + kp[:, DR_C:] * sinq[:, :DR_C]


def mla_prep(z, cos_t, sin_t, q_norm, kv_norm, w_nope, w_pe, w_rot, w_uk, cq_col, ckv_col, kpe_col):
    m = z.shape[0]
    tm = _tile(m, 256, SUBLANES)
    full = lambda a: pl.BlockSpec(a.shape, lambda i: (0,) * a.ndim)
    qn2, kvn2 = q_norm.reshape(1, DC_Q), kv_norm.reshape(1, DC_KV)
    return pl.pallas_call(
        _mla_prep_body,
        grid=(m // tm,),
        in_specs=[pl.BlockSpec((tm, DC_Q), lambda i: (i, cq_col // DC_Q)),
                  pl.BlockSpec((tm, DC_KV), lambda i: (i, ckv_col // DC_KV)),
                  pl.BlockSpec((tm, 2 * DR_C), lambda i: (i, kpe_col // (2 * DR_C))),
                  pl.BlockSpec((tm, H_C * DR_C), lambda i: (i, 0)),
                  pl.BlockSpec((tm, H_C * DR_C), lambda i: (i, 0)),
                  full(qn2), full(kvn2), full(w_nope), full(w_pe), full(w_rot), full(w_uk)],
        out_specs=[pl.BlockSpec((tm, H_C * DC_KV), lambda i: (i, 0)),
                   pl.BlockSpec((tm, H_C * DR_C), lambda i: (i, 0)),
                   pl.BlockSpec((tm, DC_KV), lambda i: (i, 0)),
                   pl.BlockSpec((tm, DR_C), lambda i: (i, 0))],
        out_shape=[jax.ShapeDtypeStruct((m, H_C * DC_KV), F32),
                   jax.ShapeDtypeStruct((m, H_C * DR_C), F32),
                   jax.ShapeDtypeStruct((m, DC_KV), F32),
                   jax.ShapeDtypeStruct((m, DR_C), F32)],
        compiler_params=_cparams("parallel"),
        name="mla_prep",
    )(z, z, z, cos_t, sin_t, qn2, kvn2, w_nope, w_pe, w_rot, w_uk)


def _q_scratch_dtype(tq):
    return BF16 if tq % BF16_ROWS == 0 else F32


def _mla_stack_q(qlat_ref, qpe_ref, qsl_ref, qsp_ref):
    tq = qlat_ref.shape[0]
    for h in range(H_C):
        qsl_ref[h * tq:(h + 1) * tq, :] = qlat_ref[:, h * DC_KV:(h + 1) * DC_KV].astype(qsl_ref.dtype)
        qsp_ref[h * tq:(h + 1) * tq, :] = qpe_ref[:, h * DR_C:(h + 1) * DR_C].astype(qsp_ref.dtype)


def _mla_finish(acc_ref, l_ref, wuv_ref, o_ref):
    tq = o_ref.shape[0]
    o = acc_ref[...] / l_ref[...]
    for h in range(H_C):
        oh = _pad_rows(o[h * tq:(h + 1) * tq], BF16_ROWS).astype(BF16)
        o_ref[:, h * DV_C:(h + 1) * DV_C] = jnp.dot(oh, wuv_ref[h], preferred_element_type=F32)[:tq]


def _diff_stack_q(dq_ref, qs_ref):
    tq = dq_ref.shape[0]
    qs_ref[...] = jnp.zeros_like(qs_ref)
    for jx in range(2):
        for r in range(H_D):
            r0 = (jx * H_D + r) * tq
            c0 = r * 2 * DH_D + jx * DH_D
            qs_ref[r0:r0 + tq, jx * DH_D:(jx + 1) * DH_D] = dq_ref[:, c0:c0 + DH_D].astype(qs_ref.dtype)


def _diff_finish(acc_ref, l_ref, lq1_ref, lk1_ref, lq2_ref, lk2_ref, sub_ref, o_ref, lam_init):
    tq = o_ref.shape[0]
    half = H_D * tq
    lam = (jnp.exp(jnp.sum(lq1_ref[...] * lk1_ref[...], axis=-1, keepdims=True))
           - jnp.exp(jnp.sum(lq2_ref[...] * lk2_ref[...], axis=-1, keepdims=True)) + lam_init)
    o = acc_ref[...] / l_ref[...]
    o = o[:half] - lam * o[half:]
    o = _rms(o, sub_ref[...], SUBLN_EPS) * (1.0 - lam_init)
    _unstack_heads(o, o_ref, H_D, 2 * DH_D)


def _causal_ok(rows, tq, tk, q0, k0):
    tok = q0 + lax.broadcasted_iota(jnp.int32, (rows, tk), 0) % tq
    key = k0 + lax.broadcasted_iota(jnp.int32, (rows, tk), 1)
    return key <= tok


def _causal_dispatch(step, q0, tq, k0, tk):
    last_key = k0 + tk - 1

    @pl.when(last_key <= q0)
    def _full():
        step(False)

    @pl.when(jnp.logical_and(last_key > q0, k0 <= q0 + tq - 1))
    def _partial():
        step(True)


def _flash_init(m_ref, l_ref, acc_ref):
    m_ref[...] = jnp.full(m_ref.shape, NEG, F32)
    l_ref[...] = jnp.zeros_like(l_ref)
    acc_ref[...] = jnp.zeros_like(acc_ref)


def _mla_prompt_body(qlat_ref, qpe_ref, lat_ref, kr_ref, wuv_ref, o_ref, qsl_ref, qsp_ref, m_ref, l_ref, acc_ref):
    i, j = pl.program_id(1), pl.program_id(2)
    tq, tk = qlat_ref.shape[0], lat_ref.shape[0]
    scale = (DN_C + DR_C) ** -0.5

    @pl.when(j == 0)
    def _init():
        _mla_stack_q(qlat_ref, qpe_ref, qsl_ref, qsp_ref)
        _flash_init(m_ref, l_ref, acc_ref)

    def step(masked):
        lat = lat_ref[...].astype(BF16)
        s = (_dot_nt(qsl_ref[...].astype(BF16), lat)
             + _dot_nt(qsp_ref[...].astype(BF16), kr_ref[...].astype(BF16))) * scale
        if masked:
            s = jnp.where(_causal_ok(H_C * tq, tq, tk, i * tq, j * tk), s, NEG)
        alpha, p = _online_softmax(s, m_ref, l_ref)
        acc_ref[...] = alpha * acc_ref[...] + jnp.dot(p.astype(BF16), lat, preferred_element_type=F32)

    _causal_dispatch(step, i * tq, tq, j * tk, tk)

    @pl.when(j == pl.num_programs(2) - 1)
    def _fin():
        _mla_finish(acc_ref, l_ref, wuv_ref, o_ref)


def mla_prompt(q_lat, q_pe, lat, kr, w_uv, batch, seq):
    tq = _tile(seq, 256, BF16_ROWS)
    tk = _tile(seq, 512, LANES)
    nq, nk = seq // tq, seq // tk
    rows = H_C * tq
    kv_idx = lambda b, i, j: (b * nk + jnp.minimum(j, (i * tq + tq - 1) // tk), 0)
    qdt = _q_scratch_dtype(tq)
    return pl.pallas_call(
        _mla_prompt_body,
        grid=(batch, nq, nk),
        in_specs=[pl.BlockSpec((tq, H_C * DC_KV), lambda b, i, j: (b * nq + i, 0)),
                  pl.BlockSpec((tq, H_C * DR_C), lambda b, i, j: (b * nq + i, 0)),
                  pl.BlockSpec((tk, DC_KV), kv_idx),
                  pl.BlockSpec((tk, DR_C), kv_idx),
                  pl.BlockSpec(w_uv.shape, lambda b, i, j: (0, 0, 0))],
        out_specs=pl.BlockSpec((tq, H_C * DV_C), lambda b, i, j: (b * nq + i, 0)),
        out_shape=jax.ShapeDtypeStruct((batch * seq, H_C * DV_C), F32),
        scratch_shapes=[pltpu.VMEM((rows, DC_KV), qdt), pltpu.VMEM((rows, DR_C), qdt),
                        pltpu.VMEM((rows, 1), F32), pltpu.VMEM((rows, 1), F32),
                        pltpu.VMEM((rows, DC_KV), F32)],
        compiler_params=_cparams("parallel", "parallel", "arbitrary"),
        name="mla_prompt",
    )(q_lat, q_pe, lat, kr, w_uv)


def _diff_prompt_body(dq_ref, dk_ref, dv_ref, lq1_ref, lk1_ref, lq2_ref, lk2_ref, sub_ref, o_ref,
                      qs_ref, m_ref, l_ref, acc_ref, *, lam_init):
    i, j = pl.program_id(1), pl.program_id(2)
    tq, tk = dq_ref.shape[0], dk_ref.shape[0]
    scale = DH_D ** -0.5

    @pl.when(j == 0)
    def _init():
        _diff_stack_q(dq_ref, qs_ref)
        _flash_init(m_ref, l_ref, acc_ref)

    def step(masked):
        s = _dot_nt(qs_ref[...].astype(BF16), dk_ref[...].astype(BF16)) * scale
        if masked:
            s = jnp.where(_causal_ok(2 * H_D * tq, tq, tk, i * tq, j * tk), s, NEG)
        alpha, p = _online_softmax(s, m_ref, l_ref)
        acc_ref[...] = alpha * acc_ref[...] + jnp.dot(p.astype(BF16), dv_ref[...].astype(BF16),
                                                      preferred_element_type=F32)

    _causal_dispatch(step, i * tq, tq, j * tk, tk)

    @pl.when(j == pl.num_programs(2) - 1)
    def _fin():
        _diff_finish(acc_ref, l_ref, lq1_ref, lk1_ref, lq2_ref, lk2_ref, sub_ref, o_ref, lam_init)


def diff_prompt(z, lams, subln, lam_init, batch, seq, dq_col, dk_col, dv_col):
    tq = _tile(seq, 256, BF16_ROWS)
    tk = _tile(seq, 512, LANES)
    nq, nk = seq // tq, seq // tk
    rows = 2 * H_D * tq
    wq, wkv = H_D * 2 * DH_D, 2 * DH_D
    kv_blk = lambda b, i, j: b * nk + jnp.minimum(j, (i * tq + tq - 1) // tk)
    vec = lambda a: pl.BlockSpec(a.shape, lambda b, i, j: (0, 0))
    return pl.pallas_call(
        functools.partial(_diff_prompt_body, lam_init=lam_init),
        grid=(batch, nq, nk),
        in_specs=[pl.BlockSpec((tq, wq), lambda b, i, j: (b * nq + i, dq_col // wq)),
                  pl.BlockSpec((tk, wkv), lambda b, i, j: (kv_blk(b, i, j), dk_col // wkv)),
                  pl.BlockSpec((tk, wkv), lambda b, i, j: (kv_blk(b, i, j), dv_col // wkv))]
                 + [vec(a) for a in lams] + [vec(subln)],
        out_specs=pl.BlockSpec((tq, wq), lambda b, i, j: (b * nq + i, 0)),
        out_shape=jax.ShapeDtypeStruct((batch * seq, wq), F32),
        scratch_shapes=[pltpu.VMEM((rows, wkv), _q_scratch_dtype(tq)), pltpu.VMEM((rows, 1), F32),
                        pltpu.VMEM((rows, 1), F32), pltpu.VMEM((rows, wkv), F32)],
        compiler_params=_cparams("parallel", "parallel", "arbitrary"),
        name="diff_prompt",
    )(z, z, z, *lams, subln)


def _new_token_init(s_new, v_new, m_ref, l_ref, acc_ref, dec_seq):
    s_new = _new_token_scores(s_new, dec_seq)
    m = jnp.max(s_new, axis=1, keepdims=True)
    p = jnp.exp(s_new - m)
    m_ref[...] = m
    l_ref[...] = jnp.sum(p, axis=1, keepdims=True)
    acc_ref[...] = _new_token_values(p, v_new)


def _mla_sample_body(pt_ref, qlat_ref, qpe_ref, nlat_ref, nkr_ref, wuv_ref, lat_hbm, kr_hbm, o_ref,
                     latbuf, krbuf, sems, qsl_ref, qsp_ref, m_ref, l_ref, acc_ref, *, pp, page0):
    j = pl.program_id(1)
    n_steps = pl.num_programs(1)
    dec_seq = qlat_ref.shape[0]
    scale = (DN_C + DR_C) ** -0.5
    streams = ((lat_hbm, PAGE_SIZE, latbuf, sems.at[0]), (kr_hbm, DR_C, krbuf, sems.at[1]))
    dma = lambda action, seq, step, slot: _page_dma(action, pt_ref, seq, step * pp, slot, pp, page0, streams)
    slot = _paged_prefetch(dma, n_steps)

    @pl.when(j == 0)
    def _init():
        _mla_stack_q(qlat_ref, qpe_ref, qsl_ref, qsp_ref)
        nlat = nlat_ref[...]
        s_new = (_dot_nt(qsl_ref[...].astype(BF16), _pad_rows(nlat, LANES).astype(BF16))
                 + _dot_nt(qsp_ref[...].astype(BF16), _pad_rows(nkr_ref[...], LANES).astype(BF16))) * scale
        _new_token_init(s_new, nlat, m_ref, l_ref, acc_ref, dec_seq)

    dma("wait", pl.program_id(0), j, slot)
    lat = latbuf[slot].reshape(pp * PAGE_SIZE, DC_KV).astype(BF16)
    kr_t = jnp.concatenate([krbuf[slot, u] for u in range(pp)], axis=1).astype(BF16)
    s = (_dot_nt(qsl_ref[...].astype(BF16), lat)
         + jnp.dot(qsp_ref[...].astype(BF16), kr_t, preferred_element_type=F32)) * scale
    alpha, p = _online_softmax(s, m_ref, l_ref)
    acc_ref[...] = alpha * acc_ref[...] + jnp.dot(p.astype(BF16), lat, preferred_element_type=F32)

    @pl.when(j == n_steps - 1)
    def _fin():
        _mla_finish(acc_ref, l_ref, wuv_ref, o_ref)


def _page_rows(cache, rows_per_page):
    n = cache.shape[0] * cache.shape[1] * rows_per_page
    return cache.reshape(n, cache.size // n)


def mla_sample(q_lat, q_pe, lat, kr, w_uv, cache_lat, cache_kr, li, page_table, row0, dec_seq):
    n_seq, n_pages = page_table.shape
    pp = _tile(n_pages, 32, 1)
    rb0 = row0 // dec_seq
    rows = H_C * dec_seq
    qrow = lambda s, j, pt: (rb0 + s, 0)
    cache_kr_t = jnp.swapaxes(cache_kr, 2, 3)
    qdt = _q_scratch_dtype(dec_seq)
    hbm = pl.BlockSpec(memory_space=pl.ANY)
    grid_spec = pltpu.PrefetchScalarGridSpec(
        num_scalar_prefetch=1,
        grid=(n_seq, n_pages // pp),
        in_specs=[pl.BlockSpec((dec_seq, H_C * DC_KV), qrow),
                  pl.BlockSpec((dec_seq, H_C * DR_C), qrow),
                  pl.BlockSpec((dec_seq, DC_KV), qrow),
                  pl.BlockSpec((dec_seq, DR_C), qrow),
                  pl.BlockSpec(w_uv.shape, lambda s, j, pt: (0, 0, 0)), hbm, hbm],
        out_specs=pl.BlockSpec((dec_seq, H_C * DV_C), lambda s, j, pt: (s, 0)),
        scratch_shapes=[pltpu.VMEM((2, pp, PAGE_SIZE, DC_KV), F32), pltpu.VMEM((2, pp, DR_C, PAGE_SIZE), F32),
                        pltpu.SemaphoreType.DMA((2, 2)),
                        pltpu.VMEM((rows, DC_KV), qdt), pltpu.VMEM((rows, DR_C), qdt),
                        pltpu.VMEM((rows, 1), F32), pltpu.VMEM((rows, 1), F32),
                        pltpu.VMEM((rows, DC_KV), F32)],
    )
    return pl.pallas_call(
        functools.partial(_mla_sample_body, pp=pp, page0=li * cache_lat.shape[1]),
        grid_spec=grid_spec,
        out_shape=jax.ShapeDtypeStruct((n_seq * dec_seq, H_C * DV_C), F32),
        compiler_params=_cparams("arbitrary", "arbitrary"),
        name="mla_sample",
    )(page_table, q_lat, q_pe, lat, kr, w_uv, _page_rows(cache_lat, PAGE_SIZE), _page_rows(cache_kr_t, DR_C))


def _deinterleaved(buf, slot, pp):
    halves = [buf[slot, :, pl.ds(c, PAGE_SIZE, stride=2), :].reshape(pp * PAGE_SIZE, DH_D) for c in range(2)]
    return jnp.concatenate(halves, axis=1)


def _diff_sample_body(pt_ref, dq_ref, nk_ref, nv_ref, lq1_ref, lk1_ref, lq2_ref, lk2_ref, sub_ref, k_hbm, v_hbm,
                      o_ref, kbuf, vbuf, sems, qs_ref, m_ref, l_ref, acc_ref, *, pp, page0, lam_init):
    j = pl.program_id(1)
    n_steps = pl.num_programs(1)
    dec_seq = dq_ref.shape[0]
    scale = DH_D ** -0.5
    rows_per_page = 2 * PAGE_SIZE
    streams = ((k_hbm, rows_per_page, kbuf, sems.at[0]), (v_hbm, rows_per_page, vbuf, sems.at[1]))
    dma = lambda action, seq, step, slot: _page_dma(action, pt_ref, seq, step * pp, slot, pp, page0, streams)
    slot = _paged_prefetch(dma, n_steps)

    @pl.when(j == 0)
    def _init():
        _diff_stack_q(dq_ref, qs_ref)
        s_new = _dot_nt(qs_ref[...].astype(BF16), _pad_rows(nk_ref[...], LANES).astype(BF16)) * scale
        _new_token_init(s_new, nv_ref[...], m_ref, l_ref, acc_ref, dec_seq)

    dma("wait", pl.program_id(0), j, slot)
    k = _deinterleaved(kbuf, slot, pp).astype(BF16)
    v = _deinterleaved(vbuf, slot, pp).astype(BF16)
    s = _dot_nt(qs_ref[...].astype(BF16), k) * scale
    alpha, p = _online_softmax(s, m_ref, l_ref)
    acc_ref[...] = alpha * acc_ref[...] + jnp.dot(p.astype(BF16), v, preferred_element_type=F32)

    @pl.when(j == n_steps - 1)
    def _fin():
        _diff_finish(acc_ref, l_ref, lq1_ref, lk1_ref, lq2_ref, lk2_ref, sub_ref, o_ref, lam_init)


def diff_sample(z, lams, subln, lam_init, cache_k, cache_v, li, page_table, row0, dec_seq, dq_col, dk_col, dv_col):
    n_seq, n_pages = page_table.shape
    pp = _tile(n_pages, 32, 1)
    rb0 = row0 // dec_seq
    rows = 2 * H_D * dec_seq
    wq, wkv = H_D * 2 * DH_D, 2 * DH_D
    vec = lambda a: pl.BlockSpec(a.shape, lambda s, j, pt: (0, 0))
    hbm = pl.BlockSpec(memory_space=pl.ANY)
    grid_spec = pltpu.PrefetchScalarGridSpec(
        num_scalar_prefetch=1,
        grid=(n_seq, n_pages // pp),
        in_specs=[pl.BlockSpec((dec_seq, wq), lambda s, j, pt: (rb0 + s, dq_col // wq)),
                  pl.BlockSpec((dec_seq, wkv), lambda s, j, pt: (rb0 + s, dk_col // wkv)),
                  pl.BlockSpec((dec_seq, wkv), lambda s, j, pt: (rb0 + s, dv_col // wkv))]
                 + [vec(a) for a in lams] + [vec(subln), hbm, hbm],
        out_specs=pl.BlockSpec((dec_seq, wq), lambda s, j, pt: (s, 0)),
        scratch_shapes=[pltpu.VMEM((2, pp, 2 * PAGE_SIZE, DH_D), F32), pltpu.VMEM((2, pp, 2 * PAGE_SIZE, DH_D), F32),
                        pltpu.SemaphoreType.DMA((2, 2)),
                        pltpu.VMEM((rows, wkv), _q_scratch_dtype(dec_seq)), pltpu.VMEM((rows, 1), F32),
                        pltpu.VMEM((rows, 1), F32), pltpu.VMEM((rows, wkv), F32)],
    )
    return pl.pallas_call(
        functools.partial(_diff_sample_body, pp=pp, page0=li * cache_k.shape[1], lam_init=lam_init),
        grid_spec=grid_spec,
        out_shape=jax.ShapeDtypeStruct((n_seq * dec_seq, wq), F32),
        compiler_params=_cparams("arbitrary", "arbitrary"),
        name="diff_sample",
    )(page_table, z, z, z, *lams, subln, _page_rows(cache_k, 2 * PAGE_SIZE), _page_rows(cache_v, 2 * PAGE_SIZE))


def _even_layer(x, li, w_in, w_out, lb_raw, gnorm, g_mix, cache_k, cache_v, state, page_table, dims):
    batch, seq, n_seq, dec_seq = dims
    n_p = batch * seq
    wa, wb = H_A * DH_A, H_B * DK_B
    n_a = wa + 2 * DH_A
    w_cat = jnp.concatenate([w_in[:, n_a:], w_in[:, :n_a]], axis=1).astype(BF16)
    b_cols = tuple(u * wb for u in range(4))
    q_off = 4 * wb
    k_off, v_off = q_off + wa, q_off + wa + DH_A
    z = norm_matmul(x, g_mix, w_cat)
    means = moba_means(z, batch, seq, k_off)
    oa_p = moba_prompt(z, means, batch, seq, q_off, k_off, v_off)
    oa_s = moba_sample(z, cache_k, cache_v, li, page_table, n_p, dec_seq, q_off, k_off, v_off)
    ob_p, st_p = hgrn2(z, lb_raw, gnorm, li, n=batch, t=seq, row0=0, cols=b_cols, state=None)
    ob_s, st_s = hgrn2(z, lb_raw, gnorm, li, n=n_seq, t=dec_seq, row0=n_p, cols=b_cols, state=state[li])
    oa = jnp.concatenate([oa_p, oa_s], axis=0)
    ob = jnp.concatenate([ob_p, ob_s], axis=0)
    x = proj_residual(x, oa, ob, w_out.astype(BF16))
    ak, av = z[:, k_off:k_off + DH_A], z[:, v_off:v_off + DH_A]
    return x, (ak, av, st_p, st_s)


def _rope_tables(pos):
    half = DR_C // 2
    inv = ROPE_THETA ** (-jnp.arange(half, dtype=F32) / half)
    ang = pos.astype(F32)[:, None] * inv[None, :]
    cos = jnp.tile(jnp.cos(ang), (1, 2 * H_C))
    sin = jnp.tile(jnp.sin(ang), (1, 2 * H_C))
    return cos, sin


def _rot_cols(w):
    half = DR_C // 2
    return jnp.concatenate([-w[..., half:], w[..., :half]], axis=-1)


def _odd_layer(x, l, li, w_in, w_out, q_norm, kv_norm, w_uq, w_uk, w_uv, lams, subln, g_mix,
               cache_cl, cache_cr, cache_dk, cache_dv, page_table, pos, dims):
    batch, seq, n_seq, dec_seq = dims
    n_p = batch * seq
    lam_init = 0.8 - 0.6 * math.exp(-0.3 * l)
    wq, wkv = H_D * 2 * DH_D, 2 * DH_D
    o_kpe = DC_Q + DC_KV
    o_dq = o_kpe + DR_C
    w_kpe = w_in[:, o_kpe:o_dq]
    w_cat = jnp.concatenate([w_in[:, :o_kpe], w_in[:, o_dq:], w_kpe, _rot_cols(w_kpe)], axis=1).astype(BF16)
    n_cq, n_ckv, n_dq = 0, DC_Q, DC_Q + DC_KV
    n_dk, n_dv, n_kpe = n_dq + wq, n_dq + wq + wkv, n_dq + wq + 2 * wkv
    z = norm_matmul(x, g_mix, w_cat)

    w3 = w_uq.reshape(DC_Q, H_C, DN_C + DR_C)
    w_nope = w3[:, :, :DN_C].reshape(DC_Q, H_C * DN_C).astype(BF16)
    w_pe3 = w3[:, :, DN_C:]
    w_pe = w_pe3.reshape(DC_Q, H_C * DR_C).astype(BF16)
    w_rot = _rot_cols(w_pe3).reshape(DC_Q, H_C * DR_C).astype(BF16)
    w_uk_t = jnp.transpose(w_uk, (1, 2, 0)).astype(BF16)
    w_uv_t = jnp.transpose(w_uv, (1, 0, 2)).astype(BF16)
    cos_t, sin_t = _rope_tables(pos)
    q_lat, q_pe, lat, kr = mla_prep(z, cos_t, sin_t, q_norm, kv_norm, w_nope, w_pe, w_rot, w_uk_t,
                                    n_cq, n_ckv, n_kpe)
    oc_p = mla_prompt(q_lat, q_pe, lat, kr, w_uv_t, batch, seq)
    oc_s = mla_sample(q_lat, q_pe, lat, kr, w_uv_t, cache_cl, cache_cr, li, page_table, n_p, dec_seq)
    lams2 = tuple(a.reshape(1, DH_D) for a in lams)
    sub2 = subln.reshape(1, 2 * DH_D)
    od_p = diff_prompt(z, lams2, sub2, lam_init, batch, seq, n_dq, n_dk, n_dv)
    od_s = diff_sample(z, lams2, sub2, lam_init, cache_dk, cache_dv, li, page_table, n_p, dec_seq,
                       n_dq, n_dk, n_dv)
    oc = jnp.concatenate([oc_p, oc_s], axis=0)
    od = jnp.concatenate([od_p, od_s], axis=0)
    x = proj_residual(x, oc, od, w_out.astype(BF16))
    return x, (lat, kr, z[:, n_dk:n_dk + wkv], z[:, n_dv:n_dv + wkv])


def kernel(x_prompt, x_sample, cache_a_k, cache_a_v, state_b, cache_c_latent, cache_c_krope, cache_d_k, cache_d_v,
           page_table, norm_mix, norm_ffn, norm_final, w_in_even, w_out_even, hgrn_lb, hgrn_gnorm, w_in_odd,
           w_out_odd, mla_q_norm, mla_kv_norm, w_mla_uq, w_mla_uk, w_mla_uv, diff_lq1, diff_lk1, diff_lq2,
           diff_lk2, diff_subln, w_ffn_gate, w_ffn_up, w_ffn_down):
    batch, seq, d = x_prompt.shape
    n_seq, dec_seq, _ = x_sample.shape
    depth = norm_mix.shape[0]
    n_p = batch * seq
    past_len = page_table.shape[1] * PAGE_SIZE
    dims = (batch, seq, n_seq, dec_seq)
    pos = jnp.concatenate([jnp.tile(jnp.arange(seq, dtype=jnp.int32), batch),
                           jnp.tile(past_len + jnp.arange(dec_seq, dtype=jnp.int32), n_seq)])
    x = jnp.concatenate([x_prompt.reshape(n_p, d), x_sample.reshape(n_seq * dec_seq, d)], axis=0)
    even, odd = [], []
    for l in range(depth):
        li = l // 2
        if l % 2 == 0:
            x, st = _even_layer(x, li, w_in_even[li], w_out_even[li], hgrn_lb, hgrn_gnorm[li], norm_mix[l],
                                cache_a_k, cache_a_v, state_b, page_table, dims)
            even.append(st)
        else:
            lams = (diff_lq1[li], diff_lk1[li], diff_lq2[li], diff_lk2[li])
            x, st = _odd_layer(x, l, li, w_in_odd[li], w_out_odd[li], mla_q_norm[li], mla_kv_norm[li],
                               w_mla_uq[li], w_mla_uk[li], w_mla_uv[li], lams, diff_subln[li], norm_mix[l],
                               cache_c_latent, cache_c_krope, cache_d_k, cache_d_v, page_table, pos, dims)
            odd.append(st)
        x = ffn(x, norm_ffn[l], norm_final, w_ffn_gate[l].astype(BF16), w_ffn_up[l].astype(BF16),
                w_ffn_down[l].astype(BF16), final_norm=(l == depth - 1))

    def split(a, tail):
        return a[:n_p].reshape((batch, seq) + tail), a[n_p:].reshape((n_seq, dec_seq) + tail)

    stack = lambda xs: jnp.stack(xs, axis=0)
    y_p, y_s = split(x, (d,))
    ak = [split(e[0], (1, DH_A)) for e in even]
    av = [split(e[1], (1, DH_A)) for e in even]
    cl = [split(o[0], (DC_KV,)) for o in odd]
    cr = [split(o[1], (DR_C,)) for o in odd]
    dk = [split(o[2], (1, 2 * DH_D)) for o in odd]
    dv = [split(o[3], (1, 2 * DH_D)) for o in odd]
    pick = lambda pairs, u: stack([p[u] for p in pairs])
    return (y_p, y_s,
            pick(ak, 0), pick(av, 0), stack([e[2] for e in even]),
            pick(cl, 0), pick(cr, 0), pick(dk, 0), pick(dv, 0),
            pick(ak, 1), pick(av, 1), stack([e[3] for e in even]),
            pick(cl, 1), pick(cr, 1), pick(dk, 1), pick(dv, 1))
```
